```python
import jax, jax.numpy as jnp
from jax import lax
import numpy as np

D_MODEL = 2048
BATCH = 1
SEQ = 16384
DEPTH = 1

CTX_LEN = 256
GRID_W = 64
A_HEADS = 8
A_DK = 128
A_DV = 128
A_WIDTH = A_HEADS * A_DK
B_WIDTH = 1024
CONV_W = 3
D_FF = 5632
CHUNK = 64
EPS = 1e-6
REC_COLS = 4 * A_WIDTH
IN_COLS = REC_COLS + A_WIDTH + 3 * B_WIDTH + 2 * D_MODEL

kernel_name = "hgrn2_shortconv_gated_hybrid_dit"


def rms_norm(x, g):
    x32 = x.astype(jnp.float32)
    y = x32 * lax.rsqrt(jnp.mean(x32 * x32, axis=-1, keepdims=True) + EPS)
    return (y * g.astype(jnp.float32)).astype(x.dtype)


def modulate(h, shift, scale):
    return h * (1 + scale) + shift


def short_conv_1d(u, w):
    up = jnp.pad(u, ((0, 0), (1, 1), (0, 0)))
    return w[0] * up[:, :-2] + w[1] * up[:, 1:-1] + w[2] * up[:, 2:]


def dwconv_grid(u, w, bias):
    bn, L, F = u.shape
    rows = L // GRID_W
    u4 = u.reshape(bn, rows, GRID_W, F)
    y = lax.conv_general_dilated(u4, w[:, :, None, :].astype(u.dtype), (1, 1), 'SAME',
                                 dimension_numbers=('NHWC', 'HWIO', 'NHWC'), feature_group_count=F)
    return y.reshape(bn, L, F) + bias


def hgrn2_chunk_scan(q, k, v, logf, s0):
    bn, L, H, _ = q.shape
    nc = L // CHUNK

    def to_chunks(t):
        return jnp.moveaxis(t.reshape(bn, nc, CHUNK, H, t.shape[-1]), 1, 0)

    causal = jnp.tril(jnp.ones((CHUNK, CHUNK), dtype=bool))[None, :, :, None, None]

    def step(S, inp):
        qc, kc, vc, gc = inp
        b = jnp.cumsum(gc, axis=1)
        o_inter = jnp.einsum('bchk,bhkv->bchv', qc * jnp.exp(b), S)
        decay = jnp.exp(jnp.where(causal, b[:, :, None] - b[:, None, :], -jnp.inf))
        scores = jnp.einsum('bthk,bshk,btshk->btsh', qc, kc, decay)
        o_intra = jnp.einsum('btsh,bshv->bthv', scores, vc)
        b_last = b[:, -1]
        k_dec = kc * jnp.exp(b_last[:, None] - b)
        S = jnp.exp(b_last)[..., None] * S + jnp.einsum('bshk,bshv->bhkv', k_dec, vc)
        return S, o_inter + o_intra

    s_fin, o = lax.scan(step, s0, (to_chunks(q), to_chunks(k), to_chunks(v), to_chunks(logf)))
    o = jnp.moveaxis(o, 0, 1).reshape(bn, L, H, v.shape[-1])
    return o, s_fin


def token_mixer(h, w_in, lb_f, lb_b, a_norm_g, sconv_w, w_pa, w_pb, w_o, s0_f, s0_b, need_output):
    bn, L, _ = h.shape
    heads = lambda t: t.reshape(bn, L, A_HEADS, -1)
    z_rec = h @ w_in[:, :REC_COLS]
    q, zf, zb, v = jnp.split(z_rec, 4, axis=-1)
    q = heads(jax.nn.silu(q.astype(jnp.float32)))
    v = heads(v.astype(jnp.float32))

    def direction(zg, lb, s0, flip):
        f = lb + (1 - lb) * jax.nn.sigmoid(zg.astype(jnp.float32))
        qq, kk, vv, lf = q, heads(1 - f), v, heads(jnp.log(f))
        if flip:
            qq, kk, vv, lf = (jnp.flip(t, axis=1) for t in (qq, kk, vv, lf))
        o, s = hgrn2_chunk_scan(qq, kk, vv, lf, s0)
        return (jnp.flip(o, axis=1) if flip else o), s

    o_f, s_f = direction(zf, lb_f, s0_f, False)
    o_b, s_b = direction(zb, lb_b, s0_b, True)
    if not need_output:
        return None, s_f, s_b

    z_rest = h @ w_in[:, REC_COLS:]
    g, sb, sc, sh, ga, gb = jnp.split(
        z_rest, np.cumsum([A_WIDTH, B_WIDTH, B_WIDTH, B_WIDTH, D_MODEL])[:].tolist(), axis=-1)
    o = o_f + o_b
    o = o * lax.rsqrt(jnp.mean(o * o, axis=-1, keepdims=True) + EPS) * a_norm_g.astype(jnp.float32)
    y_a = (o.reshape(bn, L, A_WIDTH) * jax.nn.silu(g.astype(jnp.float32))).astype(h.dtype)
    y_b = sb * short_conv_1d(sc * sh, sconv_w)
    merged = jax.nn.sigmoid(ga) * (y_a @ w_pa) + jax.nn.sigmoid(gb) * (y_b @ w_pb)
    return merged @ w_o, s_f, s_b


def conv_ffn(h, w_up, dw, db, w_down, on_grid):
    a, b = jnp.split(h @ w_up, 2, axis=-1)
    a = dwconv_grid(a, dw, db) if on_grid else short_conv_1d(a, dw[1]) + db
    return (jax.nn.silu(a) * b) @ w_down


def setup_inputs(seed: int = 0) -> dict:
    key = jax.random.key(seed)
    ks = jax.random.split(key, 20)
    nrm = lambda k, shape, s: jax.random.normal(k, shape, jnp.float32) * s
    D = D_MODEL
    return {
        "x": nrm(ks[0], (BATCH, SEQ, D), 1.0),
        "c": nrm(ks[1], (BATCH, D), 1.0),
        "ctx": nrm(ks[2], (BATCH, CTX_LEN, D), 1.0),
        "c_ctx": nrm(ks[3], (D,), 1.0),
        "w_mod": nrm(ks[4], (DEPTH, D, 6 * D), 0.5 * D ** -0.5),
        "b_mod": nrm(ks[5], (DEPTH, 6 * D), 0.02),
        "norm1_g": 1.0 + nrm(ks[6], (DEPTH, D), 0.02),
        "w_in": nrm(ks[7], (DEPTH, D, IN_COLS), D ** -0.5),
        "lb_raw": nrm(ks[8], (DEPTH + 1, 2, A_WIDTH), 0.5),
        "a_norm_g": 1.0 + nrm(ks[9], (DEPTH, A_DV), 0.02),
        "sconv_w": nrm(ks[10], (DEPTH, CONV_W, B_WIDTH), CONV_W ** -0.5),
        "w_pa": nrm(ks[11], (DEPTH, A_WIDTH, D), A_WIDTH ** -0.5),
        "w_pb": nrm(ks[12], (DEPTH, B_WIDTH, D), B_WIDTH ** -0.5),
        "w_o": nrm(ks[13], (DEPTH, D, D), D ** -0.5),
        "norm2_g": 1.0 + nrm(ks[14], (DEPTH, D), 0.02),
        "w_up": nrm(ks[15], (DEPTH, D, 2 * D_FF), D ** -0.5),
        "ffn_dw": nrm(ks[16], (DEPTH, 3, 3, D_FF), 1.0 / 3.0),
        "ffn_db": nrm(ks[17], (DEPTH, D_FF), 0.02),
        "w_down": nrm(ks[18], (DEPTH, D_FF, D), D_FF ** -0.5),
        "final_g": 1.0 + nrm(ks[19], (D,), 0.02),
    }


def reference(x, c, ctx, c_ctx, w_mod, b_mod, norm1_g, w_in, lb_raw, a_norm_g, sconv_w,
              w_pa, w_pb, w_o, norm2_g, w_up, ffn_dw, ffn_db, w_down, final_g):
    bn = x.shape[0]
    D = D_MODEL
    lbs = jnp.cumsum(jax.nn.softmax(lb_raw.astype(jnp.float32), axis=0), axis=0)
    zero_state = jnp.zeros((bn, A_HEADS, A_DK, A_DV), jnp.float32)
    silu_c = jax.nn.silu(c)
    silu_cc = jax.nn.silu(c_ctx)
    h_ctx = ctx
    for l in range(DEPTH):
        last = l == DEPTH - 1
        mod = silu_c @ w_mod[l] + b_mod[l]
        sh1, sc1, g1, sh2, sc2, g2 = (t[:, None, :] for t in jnp.split(mod, 6, axis=-1))
        n_ctx_mod = 2 if last else 6
        mod_c = silu_cc @ w_mod[l][:, :n_ctx_mod * D] + b_mod[l][:n_ctx_mod * D]
        mc = jnp.split(mod_c, n_ctx_mod)
        mix_args = (w_in[l], lbs[l, 0], lbs[l, 1], a_norm_g[l], sconv_w[l], w_pa[l], w_pb[l], w_o[l])
        hc = modulate(rms_norm(h_ctx, norm1_g[l]), mc[0], mc[1])
        out_c, s_f, s_b = token_mixer(hc, *mix_args, zero_state, zero_state, not last)
        hx = modulate(rms_norm(x, norm1_g[l]), sh1, sc1)
        out_x, _, _ = token_mixer(hx, *mix_args, s_f, s_b, True)
        x = x + g1 * out_x
        x = x + g2 * conv_ffn(modulate(rms_norm(x, norm2_g[l]), sh2, sc2),
                              w_up[l], ffn_dw[l], ffn_db[l], w_down[l], True)
        if not last:
            h_ctx = h_ctx + mc[2] * out_c
            h_ctx = h_ctx + mc[5] * conv_ffn(modulate(rms_norm(h_ctx, norm2_g[l]), mc[3], mc[4]),
                                             w_up[l], ffn_dw[l], ffn_db[l], w_down[l], False)
    return rms_norm(x, final_g)
```

```python
import functools

import jax
import jax.numpy as jnp
from jax import lax
from jax.experimental import pallas as pl
from jax.experimental.pallas import tpu as pltpu

D_MODEL = 2048
N_HEADS = 8
HEAD_DIM = 128
A_WIDTH = N_HEADS * HEAD_DIM
B_WIDTH = 1024
D_FF = 5632
CHUNK = 64
GRID_W = 64
EPS = 1e-6
SEG = 1024
N_SEG_FULL = 12
N_SEG_REC = 4
VMEM_LIMIT = 56 * 1024 * 1024
BF16 = jnp.bfloat16
F32 = jnp.float32


def _sigmoid(z):
    return 1.0 / (1.0 + jnp.exp(-z))


def _silu(z):
    return z * _sigmoid(z)


def _const_spec(shape):
    nd = len(shape)
    return pl.BlockSpec(shape, lambda *_: (0,) * nd, pipeline_mode=pl.Buffered(1))


def _mod_kernel(cc_ref, w_ref, b_ref, o_ref):
    s = _silu(cc_ref[...]).astype(BF16)
    o_ref[...] = jnp.dot(s, w_ref[...].astype(BF16), preferred_element_type=F32) + b_ref[...]


def _modulation(cc, w_mod, b_mod):
    rows, d = cc.shape
    n = w_mod.shape[1]
    tn = 512
    return pl.pallas_call(
        _mod_kernel,
        grid=(n // tn,),
        in_specs=[pl.BlockSpec((rows, d), lambda j: (0, 0)),
                  pl.BlockSpec((d, tn), lambda j: (0, j)),
                  pl.BlockSpec((1, tn), lambda j: (0, j))],
        out_specs=pl.BlockSpec((rows, tn), lambda j: (0, j)),
        out_shape=jax.ShapeDtypeStruct((rows, n), F32),
        compiler_params=pltpu.CompilerParams(dimension_semantics=("arbitrary",), vmem_limit_bytes=VMEM_LIMIT),
        name="modulation",
    )(cc, w_mod, b_mod)


def _lb_kernel(raw_ref, o_ref):
    raw = raw_ref[...]
    e = jnp.exp(raw - jnp.max(raw, axis=0, keepdims=True))
    o_ref[...] = e[0:1, :] / jnp.sum(e, axis=0, keepdims=True)


def _lower_bounds(lb_raw2d):
    return pl.pallas_call(
        _lb_kernel,
        out_shape=jax.ShapeDtypeStruct((1, lb_raw2d.shape[1]), F32),
        name="lower_bounds",
    )(lb_raw2d)


def _inproj_kernel(x_ref, sh_ref, sc_ref, g_ref, w_ref, lb_ref, o_ref, hx_ref, cs_ref, *, mod_row, n_seg):
    j = pl.program_id(1)

    @pl.when(j == 0)
    def _():
        x = x_ref[...]
        y = x * lax.rsqrt(jnp.mean(x * x, axis=-1, keepdims=True) + EPS) * g_ref[...]
        sh = sh_ref[mod_row:mod_row + 1, :]
        sc = sc_ref[mod_row:mod_row + 1, :]
        hx_ref[...] = (y * (1.0 + sc) + sh).astype(BF16)

    z = jnp.dot(hx_ref[...], w_ref[...], preferred_element_type=F32)

    def gate(lb):
        return jnp.log(lb + (1.0 - lb) * _sigmoid(z))

    @pl.when(j == 0)
    def _():
        o_ref[...] = _silu(z)

    @pl.when(j == 1)
    def _():
        o_ref[...] = gate(lb_ref[:, 0:SEG])

    @pl.when(j == 2)
    def _():
        o_ref[...] = gate(lb_ref[:, SEG:2 * SEG])

    @pl.when(j == 3)
    def _():
        o_ref[...] = z

    if n_seg > N_SEG_REC:
        @pl.when(j == 4)
        def _():
            o_ref[...] = _silu(z)

        @pl.when(j == 5)
        def _():
            o_ref[...] = z

        @pl.when(j == 6)
        def _():
            cs_ref[...] = z
            o_ref[...] = z

        @pl.when(j == 7)
        def _():
            o_ref[...] = cs_ref[...] * z

        @pl.when(j >= 8)
        def _():
            o_ref[...] = _sigmoid(z)


def _inproj(x2d, mod, g, w, lb, *, mod_row, n_seg, tm):
    L, d = x2d.shape
    kern = functools.partial(_inproj_kernel, mod_row=mod_row, n_seg=n_seg)
    return pl.pallas_call(
        kern,
        grid=(L // tm, n_seg),
        in_specs=[pl.BlockSpec((tm, d), lambda i, j: (i, 0)),
                  pl.BlockSpec((8, d), lambda i, j: (0, 0)),
                  pl.BlockSpec((8, d), lambda i, j: (0, 1)),
                  pl.BlockSpec((1, d), lambda i, j: (0, 0)),
                  pl.BlockSpec((d, SEG), lambda i, j: (0, j)),
                  pl.BlockSpec((1, 2 * SEG), lambda i, j: (0, 0))],
        out_specs=pl.BlockSpec((tm, SEG), lambda i, j: (i, j)),
        out_shape=jax.ShapeDtypeStruct((L, n_seg * SEG), F32),
        scratch_shapes=[pltpu.VMEM((tm, d), BF16), pltpu.VMEM((tm, SEG), F32)],
        compiler_params=pltpu.CompilerParams(dimension_semantics=("arbitrary", "arbitrary"),
                                             vmem_limit_bytes=VMEM_LIMIT),
        name=f"inproj_{n_seg}",
    )(x2d, mod, mod, g, w, lb)


LEVEL_HALVES = (32, 16, 8, 4, 2, 1)


def _scan_consts():
    t = jnp.arange(CHUNK)
    tril = (t[:, None] >= t[None, :])
    cums = jnp.stack([tril, tril.T]).astype(BF16)
    masks = []
    for half in LEVEL_HALVES:
        blk = 2 * half
        same = (t[:, None] // blk) == (t[None, :] // blk)
        fwd = same & ((t[:, None] % blk) >= half) & ((t[None, :] % blk) < half)
        masks.append(jnp.stack([fwd, fwd.T]))
    masks = jnp.stack(masks, axis=1).astype(F32)
    return cums, masks


def _block_reference_rows(b, half, ridx):
    blk = 2 * half
    n = CHUNK // blk
    width = b.shape[1]
    if blk >= 8:
        parts = [jnp.broadcast_to(b[i * blk + ridx:i * blk + ridx + 1, :], (blk, width)) for i in range(n)]
        return parts[0] if n == 1 else jnp.concatenate(parts, axis=0)
    pos = lax.broadcasted_iota(jnp.int32, b.shape, 0) & (blk - 1)
    r = b
    for p in range(blk):
        delta = ridx - p
        if delta != 0:
            r = jnp.where(pos == p, pltpu.roll(b, (-delta) % CHUNK, axis=0), r)
    return r


def _scan_direction(rev, q, g, v, cum, masks_ref, s_ref, o_ref):
    d = 1 if rev else 0
    g_hi = g.astype(BF16)
    r1 = g - g_hi.astype(F32)
    g_mid = r1.astype(BF16)
    g_lo = (r1 - g_mid.astype(F32)).astype(BF16)
    b = (jnp.dot(cum, g_hi, preferred_element_type=F32) + jnp.dot(cum, g_mid, preferred_element_type=F32)
         + jnp.dot(cum, g_lo, preferred_element_type=F32))
    k = 1.0 - jnp.exp(g)
    b_end = b[0:1, :] if rev else b[CHUNK - 1:CHUNK, :]
    qb = (q * jnp.exp(b)).astype(BF16)
    kdec = (k * jnp.exp(b_end - b)).astype(BF16)
    a_row = jnp.exp(b_end)
    v16 = v.astype(BF16)

    rows = lax.broadcasted_iota(jnp.int32, b.shape, 0)
    scores = [None] * N_HEADS
    for lvl, half in enumerate(LEVEL_HALVES):
        pos = rows & (2 * half - 1)
        is_q = (pos < half) if rev else (pos >= half)
        r = _block_reference_rows(b, half, half if rev else half - 1)
        e = jnp.where(is_q, b - r, r - b)
        xl = (jnp.where(is_q, q, k) * jnp.exp(e)).astype(BF16)
        m = masks_ref[d, lvl]
        for h in range(N_HEADS):
            xh = xl[:, h * HEAD_DIM:(h + 1) * HEAD_DIM]
            a = lax.dot_general(xh, xh, (((1,), (1,)), ((), ())), preferred_element_type=F32) * m
            scores[h] = a if scores[h] is None else scores[h] + a

    for h in range(N_HEADS):
        sl = slice(h * HEAD_DIM, (h + 1) * HEAD_DIM)
        st = s_ref[d, h]
        diag = jnp.sum(q[:, sl] * k[:, sl], axis=-1, keepdims=True)
        o_inter = lax.dot_general(qb[:, sl], st.astype(BF16), (((1,), (1,)), ((), ())), preferred_element_type=F32)
        o_intra = jnp.dot(scores[h].astype(BF16), v16[:, sl], preferred_element_type=F32)
        o_ref[:, sl] = o_inter + o_intra + diag * v[:, sl]
        upd = lax.dot_general(v16[:, sl], kdec[:, sl], (((0,), (0,)), ((), ())), preferred_element_type=F32)
        s_ref[d, h] = st * a_row[:, sl] + upd


def _scan_kernel(qf_ref, gf_ref, vf_ref, qb_ref, gb_ref, vb_ref, s0_ref, cum_ref, masks_ref,
                 of_ref, ob_ref, sfin_ref, s_ref):
    c = pl.program_id(0)

    @pl.when(c == 0)
    def _():
        s_ref[...] = s0_ref[...]

    _scan_direction(False, qf_ref[...], gf_ref[...], vf_ref[...], cum_ref[0], masks_ref, s_ref, of_ref)
    _scan_direction(True, qb_ref[...], gb_ref[...], vb_ref[...], cum_ref[1], masks_ref, s_ref, ob_ref)

    @pl.when(c == pl.num_programs(0) - 1)
    def _():
        sfin_ref[...] = s_ref[...]


def _scan(zact, s0):
    L = zact.shape[0]
    nc = L // CHUNK
    cums, masks = _scan_consts()
    blk = (CHUNK, SEG)
    state_spec = pl.BlockSpec(s0.shape, lambda c: (0, 0, 0, 0))
    return pl.pallas_call(
        _scan_kernel,
        grid=(nc,),
        in_specs=[pl.BlockSpec(blk, lambda c: (c, 0)),
                  pl.BlockSpec(blk, lambda c: (c, 1)),
                  pl.BlockSpec(blk, lambda c: (c, 3)),
                  pl.BlockSpec(blk, lambda c: (nc - 1 - c, 0)),
                  pl.BlockSpec(blk, lambda c: (nc - 1 - c, 2)),
                  pl.BlockSpec(blk, lambda c: (nc - 1 - c, 3)),
                  state_spec,
                  pl.BlockSpec(cums.shape, lambda c: (0, 0, 0)),
                  pl.BlockSpec(masks.shape, lambda c: (0, 0, 0, 0))],
        out_specs=[pl.BlockSpec(blk, lambda c: (c, 0)),
                   pl.BlockSpec(blk, lambda c: (nc - 1 - c, 0)),
                   state_spec],
        out_shape=[jax.ShapeDtypeStruct((L, A_WIDTH), F32),
                   jax.ShapeDtypeStruct((L, A_WIDTH), F32),
                   jax.ShapeDtypeStruct(s0.shape, F32)],
        scratch_shapes=[pltpu.VMEM(s0.shape, F32)],
        compiler_params=pltpu.CompilerParams(dimension_semantics=("arbitrary",), vmem_limit_bytes=VMEM_LIMIT),
        name=f"scan_{nc}",
    )(zact, zact, zact, zact, zact, zact, s0, cums, masks)


def _mixout_kernel(of_ref, ob_ref, sg_ref, sb_ref, u_ref, up_ref, un_ref, ga_ref, gb_ref, x_ref,
                   g1_ref, sh2_ref, sc2_ref, n2_ref, ag_ref, cw_ref, wpa_ref, wpb_ref, wo_ref,
                   x1_ref, h2_ref, ya_ref):
    i = pl.program_id(0)
    tm = x_ref.shape[0]
    for h in range(N_HEADS):
        sl = slice(h * HEAD_DIM, (h + 1) * HEAD_DIM)
        o = of_ref[:, sl] + ob_ref[:, sl]
        o = o * lax.rsqrt(jnp.mean(o * o, axis=-1, keepdims=True) + EPS) * ag_ref[...]
        ya_ref[:, sl] = (o * sg_ref[:, sl]).astype(BF16)
    u = u_ref[...]
    rows = lax.broadcasted_iota(jnp.int32, u.shape, 0)
    prev_row = jnp.where(i > 0, up_ref[7:8, :], 0.0)
    next_row = jnp.where(i < pl.num_programs(0) - 1, un_ref[0:1, :], 0.0)
    u_prev = jnp.where(rows == 0, prev_row, pltpu.roll(u, 1, axis=0))
    u_next = jnp.where(rows == tm - 1, next_row, pltpu.roll(u, tm - 1, axis=0))
    yb = sb_ref[...] * (cw_ref[0:1, :] * u_prev + cw_ref[1:2, :] * u + cw_ref[2:3, :] * u_next)
    pa = jnp.dot(ya_ref[...], wpa_ref[...], preferred_element_type=F32)
    pb = jnp.dot(yb.astype(BF16), wpb_ref[...], preferred_element_type=F32)
    merged = (ga_ref[...] * pa + gb_ref[...] * pb).astype(BF16)
    out = jnp.dot(merged, wo_ref[...], preferred_element_type=F32)
    x1 = x_ref[...] + g1_ref[0:1, :] * out
    x1_ref[...] = x1
    y = x1 * lax.rsqrt(jnp.mean(x1 * x1, axis=-1, keepdims=True) + EPS) * n2_ref[...]
    h2_ref[...] = (y * (1.0 + sc2_ref[0:1, :]) + sh2_ref[0:1, :]).astype(BF16)


def _mixout(o_f, o_b, zact, x2d, mod, n2g, ag, cw, wpa, wpb, wo, *, tm):
    L, d = x2d.shape
    nt = L // tm
    r8 = tm // 8
    last8 = L // 8 - 1
    row = lambda i: (i, 0)
    return pl.pallas_call(
        _mixout_kernel,
        grid=(nt,),
        in_specs=[pl.BlockSpec((tm, A_WIDTH), row),
                  pl.BlockSpec((tm, A_WIDTH), row),
                  pl.BlockSpec((tm, SEG), lambda i: (i, 4)),
                  pl.BlockSpec((tm, SEG), lambda i: (i, 5)),
                  pl.BlockSpec((tm, SEG), lambda i: (i, 7)),
                  pl.BlockSpec((8, SEG), lambda i: (jnp.maximum(i * r8 - 1, 0), 7)),
                  pl.BlockSpec((8, SEG), lambda i: (jnp.minimum((i + 1) * r8, last8), 7)),
                  pl.BlockSpec((tm, d), lambda i: (i, 4)),
                  pl.BlockSpec((tm, d), lambda i: (i, 5)),
                  pl.BlockSpec((tm, d), row),
                  pl.BlockSpec((8, d), lambda i: (0, 2)),
                  pl.BlockSpec((8, d), lambda i: (0, 3)),
                  pl.BlockSpec((8, d), lambda i: (0, 4)),
                  pl.BlockSpec((1, d), lambda i: (0, 0)),
                  pl.BlockSpec((1, HEAD_DIM), lambda i: (0, 0)),
                  pl.BlockSpec(cw.shape, lambda i: (0, 0)),
                  _const_spec(wpa.shape), _const_spec(wpb.shape), _const_spec(wo.shape)],
        out_specs=[pl.BlockSpec((tm, d), row), pl.BlockSpec((tm, d), row)],
        out_shape=[jax.ShapeDtypeStruct((L, d), F32), jax.ShapeDtypeStruct((L, d), BF16)],
        scratch_shapes=[pltpu.VMEM((tm, A_WIDTH), BF16)],
        compiler_params=pltpu.CompilerParams(dimension_semantics=("arbitrary",), vmem_limit_bytes=VMEM_LIMIT),
        name="mixer_out",
    )(o_f, o_b, zact, zact, zact, zact, zact, zact, zact, x2d, mod, mod, mod, n2g, ag, cw, wpa, wpb, wo)


FFN_PAD = 8


def _ffn_kernel(h_ref, ht_ref, hb_ref, x1_ref, wa_ref, wb_ref, wd_ref, dw_ref, db_ref, g2_ref, fg_ref,
                o_ref, ac_ref, al_ref, ar_ref):
    i = pl.program_id(0)
    f = pl.program_id(1)
    tm = h_ref.shape[0]
    tf = wa_ref.shape[1]
    wa = wa_ref[...]
    top_ok = (i > 0).astype(F32)
    bot_ok = (i < pl.num_programs(0) - 1).astype(F32)
    lo = FFN_PAD
    mid = lo + GRID_W
    hi = mid + tm
    end = hi + GRID_W
    zpad = jnp.zeros((FFN_PAD, tf), F32)
    ac_ref[0:lo, :] = zpad
    ac_ref[lo:mid, :] = jnp.dot(ht_ref[...], wa, preferred_element_type=F32) * top_ok
    ac_ref[mid:hi, :] = jnp.dot(h_ref[...], wa, preferred_element_type=F32)
    ac_ref[hi:end, :] = jnp.dot(hb_ref[...], wa, preferred_element_type=F32) * bot_ok
    ac_ref[end:end + FFN_PAD, :] = zpad
    a = ac_ref[...]
    col = (lax.broadcasted_iota(jnp.int32, a.shape, 0) + (GRID_W - FFN_PAD)) & (GRID_W - 1)
    al_ref[...] = jnp.where(col == GRID_W - 1, 0.0, a)
    ar_ref[...] = jnp.where(col == 0, 0.0, a)
    acc = None
    for dr in range(3):
        base = mid + (dr - 1) * GRID_W
        term = (dw_ref[3 * dr:3 * dr + 1, :] * al_ref[base - 1:base - 1 + tm, :]
                + dw_ref[3 * dr + 1:3 * dr + 2, :] * ac_ref[base:base + tm, :]
                + dw_ref[3 * dr + 2:3 * dr + 3, :] * ar_ref[base + 1:base + 1 + tm, :])
        acc = term if acc is None else acc + term
    av = acc + db_ref[...]
    bv = jnp.dot(h_ref[...], wb_ref[...], preferred_element_type=F32)
    hid = (_silu(av) * bv).astype(BF16)
    part = jnp.dot(hid, wd_ref[...], preferred_element_type=F32)

    @pl.when(f == 0)
    def _():
        o_ref[...] = part

    @pl.when(f > 0)
    def _():
        o_ref[...] += part

    @pl.when(f == pl.num_programs(1) - 1)
    def _():
        x2 = x1_ref[...] + g2_ref[0:1, :] * o_ref[...]
        o_ref[...] = x2 * lax.rsqrt(jnp.mean(x2 * x2, axis=-1, keepdims=True) + EPS) * fg_ref[...]


def _ffn(h2, x1, mod, w_up, w_down, dw9, db, fg, *, tm, tf):
    L, d = x1.shape
    nt = L // tm
    nf = D_FF // tf
    rg = tm // GRID_W
    lastg = L // GRID_W - 1
    ext = tm + 2 * GRID_W + 2 * FFN_PAD
    return pl.pallas_call(
        _ffn_kernel,
        grid=(nt, nf),
        in_specs=[pl.BlockSpec((tm, d), lambda i, f: (i, 0)),
                  pl.BlockSpec((GRID_W, d), lambda i, f: (jnp.maximum(i * rg - 1, 0), 0)),
                  pl.BlockSpec((GRID_W, d), lambda i, f: (jnp.minimum((i + 1) * rg, lastg), 0)),
                  pl.BlockSpec((tm, d), lambda i, f: (i, 0)),
                  pl.BlockSpec((d, tf), lambda i, f: (0, f)),
                  pl.BlockSpec((d, tf), lambda i, f: (0, nf + f)),
                  pl.BlockSpec((tf, d), lambda i, f: (f, 0)),
                  pl.BlockSpec((9, tf), lambda i, f: (0, f)),
                  pl.BlockSpec((1, tf), lambda i, f: (0, f)),
                  pl.BlockSpec((8, d), lambda i, f: (0, 5)),
                  pl.BlockSpec((1, d), lambda i, f: (0, 0))],
        out_specs=pl.BlockSpec((tm, d), lambda i, f: (i, 0)),
        out_shape=jax.ShapeDtypeStruct((L, d), F32),
        scratch_shapes=[pltpu.VMEM((ext, tf), F32), pltpu.VMEM((ext, tf), F32), pltpu.VMEM((ext, tf), F32)],
        compiler_params=pltpu.CompilerParams(dimension_semantics=("arbitrary", "arbitrary"),
                                             vmem_limit_bytes=VMEM_LIMIT),
        name="conv_ffn",
    )(h2, h2, h2, x1, w_up, w_up, w_down, dw9, db, mod, fg)


def kernel(x, c, ctx, c_ctx, w_mod, b_mod, norm1_g, w_in, lb_raw, a_norm_g, sconv_w, w_pa, w_pb, w_o,
           norm2_g, w_up, ffn_dw, ffn_db, w_down, final_g):
    assert x.shape[0] == 1 and w_in.shape[0] == 1, "single-sequence, depth-1 layer"
    d = D_MODEL
    x2d = x[0]
    ctx2d = ctx[0]
    L = x2d.shape[0]
    Lc = ctx2d.shape[0]

    cc = jnp.zeros((16, d), F32).at[0].set(c[0]).at[1].set(c_ctx)
    mod = _modulation(cc, w_mod[0], b_mod[0][None, :])
    lb = _lower_bounds(lb_raw.reshape(lb_raw.shape[0], 2 * A_WIDTH))

    w_in16 = w_in[0].astype(BF16)
    g1n = norm1_g[0][None, :]
    zc = _inproj(ctx2d, mod, g1n, w_in16, lb, mod_row=1, n_seg=N_SEG_REC, tm=min(Lc, 256))
    s_zero = jnp.zeros((2, N_HEADS, HEAD_DIM, HEAD_DIM), F32)
    _, _, s_ctx = _scan(zc, s_zero)
    zx = _inproj(x2d, mod, g1n, w_in16, lb, mod_row=0, n_seg=N_SEG_FULL, tm=min(L, 1024))
    o_f, o_b, _ = _scan(zx, s_ctx)
    x1, h2 = _mixout(o_f, o_b, zx, x2d, mod, norm2_g[0][None, :], a_norm_g[0][None, :], sconv_w[0],
                     w_pa[0].astype(BF16), w_pb[0].astype(BF16), w_o[0].astype(BF16), tm=min(L, 256))
    out = _ffn(h2, x1, mod, w_up[0].astype(BF16), w_down[0].astype(BF16),
               ffn_dw[0].reshape(9, D_FF), ffn_db[0][None, :], final_g[None, :], tm=min(L, 512), tf=512)
    return out[None]
```

```python
import functools
import math

import jax
import jax.numpy as jnp
from jax import lax
from jax.experimental import pallas as pl
from jax.experimental.pallas import tpu as pltpu

D_MODEL = 2048
N_HEADS = 8
HEAD_DIM = 128
A_WIDTH = N_HEADS * HEAD_DIM
B_WIDTH = 1024
D_FF = 5632
CHUNK = 64
GRID_W = 64
EPS = 1e-6
SEG = 1024
N_SEG_REC = 4
SEG_H = 7
REST_GATE_A, REST_GATE_B, REST_G, REST_B, REST_U, N_REST = 0, 2, 4, 5, 6, 7
V7X_VMEM_BYTES = 64 * 1024 * 1024
VMEM_LIMIT = V7X_VMEM_BYTES - 8 * 1024 * 1024
BF16_ROWS = 16
LOG2E = math.log2(math.e)
BF16 = jnp.bfloat16
F32 = jnp.float32


def _tiles(L):
    return dict(inproj_tm=min(L, 1024), mix_tm=min(L, 256), ffn_tm=min(L, 512), ffn_tf=512, ffn_sub=256,
                scan_chunks=min(L // CHUNK, 4), mod_tn=512)


def _sigmoid(z):
    return 1.0 / (1.0 + jnp.exp(-z))


def _silu(z):
    return z * _sigmoid(z)


def _const_spec(shape, index=None):
    idx = index if index is not None else (0,) * len(shape)
    return pl.BlockSpec(shape, lambda *_: idx, pipeline_mode=pl.Buffered(1))


def _params(*semantics):
    return pltpu.CompilerParams(dimension_semantics=semantics, vmem_limit_bytes=VMEM_LIMIT)


def _mod_kernel(cc_ref, w_ref, b_ref, o_ref):
    s = _silu(cc_ref[...]).astype(BF16)
    o_ref[...] = jnp.dot(s, w_ref[...].astype(BF16), preferred_element_type=F32) + b_ref[...]


def _modulation(cc, w_mod, b_mod, tn):
    rows, d = cc.shape
    n = w_mod.shape[1]
    return pl.pallas_call(
        _mod_kernel,
        grid=(n // tn,),
        in_specs=[pl.BlockSpec((rows, d), lambda j: (0, 0)),
                  pl.BlockSpec((d, tn), lambda j: (0, j)),
                  pl.BlockSpec((1, tn), lambda j: (0, j))],
        out_specs=pl.BlockSpec((rows, tn), lambda j: (0, j)),
        out_shape=jax.ShapeDtypeStruct((rows, n), F32),
        compiler_params=_params("arbitrary"),
        name="modulation",
    )(cc, w_mod, b_mod)


def _lb_kernel(raw_ref, o_ref):
    raw = raw_ref[...]
    e = jnp.exp(raw - jnp.max(raw, axis=0, keepdims=True))
    o_ref[...] = e[0:1, :] / jnp.sum(e, axis=0, keepdims=True)


def _lower_bounds(lb_raw2d):
    return pl.pallas_call(
        _lb_kernel,
        out_shape=jax.ShapeDtypeStruct((1, lb_raw2d.shape[1]), F32),
        name="lower_bounds",
    )(lb_raw2d)


def _inproj_kernel(*refs, mod_row, full):
    if full:
        x_ref, sh_ref, sc_ref, g_ref, w_ref, wh_ref, lb_ref, rec_ref, rest_ref, hx_ref = refs
    else:
        x_ref, sh_ref, sc_ref, g_ref, w_ref, lb_ref, rec_ref, hx_ref = refs
    j = pl.program_id(1)

    @pl.when(j == 0)
    def _():
        x = x_ref[...]
        y = x * lax.rsqrt(jnp.mean(x * x, axis=-1, keepdims=True) + EPS) * g_ref[...]
        sh = sh_ref[mod_row:mod_row + 1, :]
        sc = sc_ref[mod_row:mod_row + 1, :]
        hx_ref[...] = (y * (1.0 + sc) + sh).astype(BF16)

    def z(wr=w_ref):
        return jnp.dot(hx_ref[...], wr[...], preferred_element_type=F32)

    @pl.when(j == 0)
    def _():
        rec_ref[...] = _silu(z())

    @pl.when((j == 1) | (j == 2))
    def _():
        lb = jnp.where(j == 1, lb_ref[:, 0:SEG], lb_ref[:, SEG:2 * SEG])
        rec_ref[...] = jnp.log(lb + (1.0 - lb) * _sigmoid(z()))

    @pl.when(j == 3)
    def _():
        rec_ref[...] = z()

    if full:
        @pl.when(j == 4)
        def _():
            rest_ref[...] = _silu(z()).astype(BF16)

        @pl.when(j == 5)
        def _():
            rest_ref[...] = z().astype(BF16)

        @pl.when(j == 6)
        def _():
            rest_ref[...] = (z() * z(wh_ref)).astype(BF16)

        @pl.when(j >= 7)
        def _():
            rest_ref[...] = _sigmoid(z()).astype(BF16)


def _inproj(x2d, mod, g, w, lb, *, mod_row, full, tm):
    L, d = x2d.shape
    n_steps = 11 if full else N_SEG_REC
    w_seg = lambda i, j: (0, jnp.where(j >= SEG_H, j + 1, j))
    rest_blk = lambda i, j: (i, jnp.where(j < REST_G, REST_G, jnp.where(j <= REST_U, j, j - 7)))
    in_specs = [pl.BlockSpec((tm, d), lambda i, j: (i, 0)),
                pl.BlockSpec((8, d), lambda i, j: (0, 0)),
                pl.BlockSpec((8, d), lambda i, j: (0, 1)),
                pl.BlockSpec((1, d), lambda i, j: (0, 0)),
                pl.BlockSpec((d, SEG), w_seg)]
    args = [x2d, mod, mod, g, w]
    out_specs = [pl.BlockSpec((tm, SEG), lambda i, j: (i, jnp.minimum(j, N_SEG_REC - 1)))]
    out_shape = [jax.ShapeDtypeStruct((L, N_SEG_REC * SEG), F32)]
    if full:
        in_specs.append(_const_spec((d, SEG), (0, SEG_H)))
        args.append(w)
        out_specs.append(pl.BlockSpec((tm, SEG), rest_blk))
        out_shape.append(jax.ShapeDtypeStruct((L, N_REST * SEG), BF16))
    in_specs.append(pl.BlockSpec((1, 2 * SEG), lambda i, j: (0, 0)))
    args.append(lb)
    return pl.pallas_call(
        functools.partial(_inproj_kernel, mod_row=mod_row, full=full),
        grid=(L // tm, n_steps),
        in_specs=in_specs,
        out_specs=out_specs,
        out_shape=out_shape,
        scratch_shapes=[pltpu.VMEM((tm, d), BF16)],
        compiler_params=_params("arbitrary", "arbitrary"),
        name="inproj_full" if full else "inproj_rec",
    )(*args)


LEVEL_HALVES = (32, 16, 8, 4, 2, 1)


def _scan_consts():
    t = jnp.arange(CHUNK)
    tril = (t[:, None] >= t[None, :])
    cums = jnp.stack([tril, tril.T]).astype(BF16)
    masks = []
    for half in LEVEL_HALVES:
        blk = 2 * half
        same = (t[:, None] // blk) == (t[None, :] // blk)
        fwd = same & ((t[:, None] % blk) >= half) & ((t[None, :] % blk) < half)
        masks.append(jnp.stack([fwd, fwd.T]))
    masks = jnp.stack(masks, axis=1).astype(F32)
    return cums, masks


def _block_reference_rows(b, half, ridx):
    blk = 2 * half
    n = CHUNK // blk
    width = b.shape[1]
    if blk >= 8:
        parts = [jnp.broadcast_to(b[i * blk + ridx:i * blk + ridx + 1, :], (blk, width)) for i in range(n)]
        return parts[0] if n == 1 else jnp.concatenate(parts, axis=0)
    pos = lax.broadcasted_iota(jnp.int32, b.shape, 0) & (blk - 1)
    r = b
    for p in range(blk):
        delta = ridx - p
        if delta != 0:
            r = jnp.where(pos == p, pltpu.roll(b, (-delta) % CHUNK, axis=0), r)
    return r


def _neg_abs(x):
    bits = lax.bitcast_convert_type(x, jnp.uint32) | jnp.uint32(0x80000000)
    return lax.bitcast_convert_type(bits, F32)


def _query_key_rows(rev, half, q, k):
    blk = 2 * half
    if half >= 8:
        parts = []
        for i in range(CHUNK // blk):
            lo, mid, hi = i * blk, i * blk + half, (i + 1) * blk
            parts += [q[lo:mid], k[mid:hi]] if rev else [k[lo:mid], q[mid:hi]]
        return jnp.concatenate(parts, axis=0)
    pos = lax.broadcasted_iota(jnp.int32, q.shape, 0) & (blk - 1)
    return jnp.where((pos < half) if rev else (pos >= half), q, k)


def _scan_cumulative(d, g_ref, rows, cum, b_ref):
    g = g_ref[rows, :]
    g_hi = g.astype(BF16)
    r1 = g - g_hi.astype(F32)
    g_mid = r1.astype(BF16)
    g_lo = (r1 - g_mid.astype(F32)).astype(BF16)
    b_ref[d] = (jnp.dot(cum, g_hi, preferred_element_type=F32) + jnp.dot(cum, g_mid, preferred_element_type=F32)
                + jnp.dot(cum, g_lo, preferred_element_type=F32)) * LOG2E


def _scan_scores(d, h, q_ref, g_ref, v_ref, rows, masks_ref, b_ref, pos):
    rev = d == 1
    sl = slice(h * HEAD_DIM, (h + 1) * HEAD_DIM)
    end = 0 if rev else CHUNK - 1
    q = q_ref[rows, sl]
    v = v_ref[rows, sl]
    f = jnp.exp(g_ref[rows, sl])
    k = 1.0 - f
    qf = q * f
    b = b_ref[d, :, sl]
    b_end = b_ref[d, end:end + 1, sl]
    scores = None
    for lvl, half in enumerate(LEVEL_HALVES):
        if half > 2:
            r = _block_reference_rows(b, half, half if rev else half - 1)
            xl = _query_key_rows(rev, half, q, k) * jnp.exp2(_neg_abs(b - r))
        elif half == 2:
            f_prev = pltpu.roll(f, 1, axis=0)
            f_next = pltpu.roll(f, CHUNK - 1, axis=0)
            if rev:
                xl = jnp.where(pos["q4"], qf * jnp.where(pos["first4"], f_next, 1.0), k * jnp.where(pos["last4"], f_prev, 1.0))
            else:
                xl = jnp.where(pos["q4"], qf * jnp.where(pos["last4"], f_prev, 1.0), k * jnp.where(pos["first4"], f_next, 1.0))
        else:
            xl = jnp.where(pos["q2"], qf, k)
        xl = xl.astype(BF16)
        a = lax.dot_general(xl, xl, (((1,), (1,)), ((), ())), preferred_element_type=F32) * masks_ref[d, lvl]
        scores = a if scores is None else scores + a
    qb = (q * jnp.exp2(b)).astype(BF16)
    kdec = (k * jnp.exp2(b_end - b)).astype(BF16)
    own = jnp.sum(q * k, axis=-1, keepdims=True) * v
    return scores.astype(BF16), qb, kdec, v.astype(BF16), own, jnp.exp2(b_end)


def _scan_state(d, h, staged, rows, s_ref, o_ref):
    p16, qb, kdec, v16, own, decay = staged
    sl = slice(h * HEAD_DIM, (h + 1) * HEAD_DIM)
    st = s_ref[d, h]
    o_inter = lax.dot_general(qb, st.astype(BF16), (((1,), (1,)), ((), ())), preferred_element_type=F32)
    o_intra = jnp.dot(p16, v16, preferred_element_type=F32)
    o_ref[rows, sl] = o_inter + o_intra + own
    upd = lax.dot_general(v16, kdec, (((0,), (0,)), ((), ())), preferred_element_type=F32)
    s_ref[d, h] = st * decay + upd


SCAN_LAG = 2


def _scan_kernel(qf_ref, gf_ref, vf_ref, qb_ref, gb_ref, vb_ref, s0_ref, cum_ref, masks_ref,
                 of_ref, ob_ref, sfin_ref, s_ref, b_ref, *, n_sub):
    c = pl.program_id(0)

    @pl.when(c == 0)
    def _():
        s_ref[...] = s0_ref[...]

    def body(cc, carry):
        rows = (pl.ds(pl.multiple_of(cc * CHUNK, CHUNK), CHUNK),
                pl.ds(pl.multiple_of((n_sub - 1 - cc) * CHUNK, CHUNK), CHUNK))
        ins = ((qf_ref, gf_ref, vf_ref), (qb_ref, gb_ref, vb_ref))
        outs = (of_ref, ob_ref)
        for d in range(2):
            _scan_cumulative(d, ins[d][1], rows[d], cum_ref[d], b_ref)
        r4 = lax.broadcasted_iota(jnp.int32, (CHUNK, HEAD_DIM), 0) & 3
        pos = [dict(q4=(r4 < 2) if d else (r4 >= 2), first4=r4 == 0, last4=r4 == 3,
                    q2=((r4 & 1) == 0) if d else ((r4 & 1) == 1)) for d in range(2)]
        chains = [(d, h) for h in range(N_HEADS) for d in range(2)]
        staged = {}
        for n in range(len(chains) + SCAN_LAG):
            if n < len(chains):
                d, h = chains[n]
                staged[n] = _scan_scores(d, h, *ins[d], rows[d], masks_ref, b_ref, pos[d])
            if n >= SCAN_LAG:
                d, h = chains[n - SCAN_LAG]
                _scan_state(d, h, staged.pop(n - SCAN_LAG), rows[d], s_ref, outs[d])
        return carry

    lax.fori_loop(0, n_sub, body, 0)

    @pl.when(c == pl.num_programs(0) - 1)
    def _():
        sfin_ref[...] = s_ref[...]


def _scan(rec, s0, n_sub):
    L = rec.shape[0]
    ns = L // (CHUNK * n_sub)
    cums, masks = _scan_consts()
    blk = (CHUNK * n_sub, SEG)
    state_spec = pl.BlockSpec(s0.shape, lambda c: (0, 0, 0, 0))
    return pl.pallas_call(
        functools.partial(_scan_kernel, n_sub=n_sub),
        grid=(ns,),
        in_specs=[pl.BlockSpec(blk, lambda c: (c, 0)),
                  pl.BlockSpec(blk, lambda c: (c, 1)),
                  pl.BlockSpec(blk, lambda c: (c, 3)),
                  pl.BlockSpec(blk, lambda c: (ns - 1 - c, 0)),
                  pl.BlockSpec(blk, lambda c: (ns - 1 - c, 2)),
                  pl.BlockSpec(blk, lambda c: (ns - 1 - c, 3)),
                  state_spec,
                  pl.BlockSpec(cums.shape, lambda c: (0, 0, 0)),
                  pl.BlockSpec(masks.shape, lambda c: (0, 0, 0, 0))],
        out_specs=[pl.BlockSpec(blk, lambda c: (c, 0)),
                   pl.BlockSpec(blk, lambda c: (ns - 1 - c, 0)),
                   state_spec],
        out_shape=[jax.ShapeDtypeStruct((L, A_WIDTH), F32),
                   jax.ShapeDtypeStruct((L, A_WIDTH), F32),
                   jax.ShapeDtypeStruct(s0.shape, F32)],
        scratch_shapes=[pltpu.VMEM(s0.shape, F32), pltpu.VMEM((2, CHUNK, A_WIDTH), F32)],
        compiler_params=_params("arbitrary"),
        name=f"scan_{ns}",
    )(rec, rec, rec, rec, rec, rec, s0, cums, masks)


def _mixout_kernel(of_ref, ob_ref, sg_ref, sb_ref, u_ref, up_ref, un_ref, ga_ref, gb_ref, x_ref,
                   g1_ref, sh2_ref, sc2_ref, n2_ref, ag_ref, cw_ref, wpa_ref, wpb_ref, wo_ref,
                   x1_ref, h2_ref, ya_ref):
    i = pl.program_id(0)
    tm = x_ref.shape[0]
    for h in range(N_HEADS):
        sl = slice(h * HEAD_DIM, (h + 1) * HEAD_DIM)
        o = of_ref[:, sl] + ob_ref[:, sl]
        o = o * lax.rsqrt(jnp.mean(o * o, axis=-1, keepdims=True) + EPS) * ag_ref[...]
        ya_ref[:, sl] = (o * sg_ref[:, sl].astype(F32)).astype(BF16)
    u = u_ref[...].astype(F32)
    rows = lax.broadcasted_iota(jnp.int32, u.shape, 0)
    prev_row = jnp.where(i > 0, up_ref[...].astype(F32)[BF16_ROWS - 1:BF16_ROWS, :], 0.0)
    next_row = jnp.where(i < pl.num_programs(0) - 1, un_ref[...].astype(F32)[0:1, :], 0.0)
    u_prev = jnp.where(rows == 0, prev_row, pltpu.roll(u, 1, axis=0))
    u_next = jnp.where(rows == tm - 1, next_row, pltpu.roll(u, tm - 1, axis=0))
    yb = sb_ref[...].astype(F32) * (cw_ref[0:1, :] * u_prev + cw_ref[1:2, :] * u + cw_ref[2:3, :] * u_next)
    pa = jnp.dot(ya_ref[...], wpa_ref[...], preferred_element_type=F32)
    pb = jnp.dot(yb.astype(BF16), wpb_ref[...], preferred_element_type=F32)
    merged = (ga_ref[...].astype(F32) * pa + gb_ref[...].astype(F32) * pb).astype(BF16)
    out = jnp.dot(merged, wo_ref[...], preferred_element_type=F32)
    x1 = x_ref[...] + g1_ref[0:1, :] * out
    x1_ref[...] = x1
    y = x1 * lax.rsqrt(jnp.mean(x1 * x1, axis=-1, keepdims=True) + EPS) * n2_ref[...]
    h2_ref[...] = (y * (1.0 + sc2_ref[0:1, :]) + sh2_ref[0:1, :]).astype(BF16)


def _mixout(o_f, o_b, rest, x2d, mod, n2g, ag, cw, wpa, wpb, wo, *, tm):
    L, d = x2d.shape
    nt = L // tm
    rt = tm // BF16_ROWS
    last_t = L // BF16_ROWS - 1
    row = lambda i: (i, 0)
    return pl.pallas_call(
        _mixout_kernel,
        grid=(nt,),
        in_specs=[pl.BlockSpec((tm, A_WIDTH), row),
                  pl.BlockSpec((tm, A_WIDTH), row),
                  pl.BlockSpec((tm, SEG), lambda i: (i, REST_G)),
                  pl.BlockSpec((tm, SEG), lambda i: (i, REST_B)),
                  pl.BlockSpec((tm, SEG), lambda i: (i, REST_U)),
                  pl.BlockSpec((BF16_ROWS, SEG), lambda i: (jnp.maximum(i * rt - 1, 0), REST_U)),
                  pl.BlockSpec((BF16_ROWS, SEG), lambda i: (jnp.minimum((i + 1) * rt, last_t), REST_U)),
                  pl.BlockSpec((tm, d), lambda i: (i, REST_GATE_A // 2)),
                  pl.BlockSpec((tm, d), lambda i: (i, REST_GATE_B // 2)),
                  pl.BlockSpec((tm, d), row),
                  pl.BlockSpec((8, d), lambda i: (0, 2)),
                  pl.BlockSpec((8, d), lambda i: (0, 3)),
                  pl.BlockSpec((8, d), lambda i: (0, 4)),
                  pl.BlockSpec((1, d), lambda i: (0, 0)),
                  pl.BlockSpec((1, HEAD_DIM), lambda i: (0, 0)),
                  pl.BlockSpec(cw.shape, lambda i: (0, 0)),
                  _const_spec(wpa.shape), _const_spec(wpb.shape), _const_spec(wo.shape)],
        out_specs=[pl.BlockSpec((tm, d), row), pl.BlockSpec((tm, d), row)],
        out_shape=[jax.ShapeDtypeStruct((L, d), F32), jax.ShapeDtypeStruct((L, d), BF16)],
        scratch_shapes=[pltpu.VMEM((tm, A_WIDTH), BF16)],
        compiler_params=_params("arbitrary"),
        name="mixer_out",
    )(o_f, o_b, rest, rest, rest, rest, rest, rest, rest, x2d, mod, mod, mod, n2g, ag, cw, wpa, wpb, wo)


def _ffn_kernel(h_ref, ht_ref, hb_ref, x1_ref, wa_ref, wb_ref, wd_ref, dw_ref, db_ref, g2_ref, fg_ref,
                o_ref, hx_ref, *, sub):
    i = pl.program_id(0)
    f = pl.program_id(1)
    tm = h_ref.shape[0]
    tf = wa_ref.shape[1]
    ext = tm + 2 * GRID_W

    @pl.when(f == 0)
    def _():
        halo_zero = jnp.zeros(ht_ref.shape, BF16)
        hx_ref[0:GRID_W, :] = jnp.where(i > 0, ht_ref[...], halo_zero)
        hx_ref[GRID_W:GRID_W + tm, :] = h_ref[...]
        hx_ref[GRID_W + tm:ext, :] = jnp.where(i < pl.num_programs(0) - 1, hb_ref[...], halo_zero)
        o_ref[...] = jnp.zeros(o_ref.shape, F32)

    ng = tm // GRID_W
    n_sub = tf // sub
    a_all = [jnp.dot(hx_ref[...], wa_ref[:, s * sub:(s + 1) * sub], preferred_element_type=F32) for s in range(n_sub)]
    b_all = [jnp.dot(hx_ref[GRID_W:GRID_W + tm, :], wb_ref[:, s * sub:(s + 1) * sub], preferred_element_type=F32)
             for s in range(n_sub)]
    colg = lax.broadcasted_iota(jnp.int32, (GRID_W, sub), 0)
    acc = None
    for s in range(n_sub):
        cs = slice(s * sub, (s + 1) * sub)
        a = a_all[s]
        a3 = a.reshape(ng + 2, GRID_W, sub)
        l3 = pltpu.roll(a, 1, axis=0).reshape(ng + 2, GRID_W, sub)
        r3 = pltpu.roll(a, ext - 1, axis=0).reshape(ng + 2, GRID_W, sub)
        conv = None
        for dr in range(3):
            w_l = jnp.where(colg == 0, 0.0, dw_ref[3 * dr:3 * dr + 1, cs])
            w_c = dw_ref[3 * dr + 1:3 * dr + 2, cs]
            w_r = jnp.where(colg == GRID_W - 1, 0.0, dw_ref[3 * dr + 2:3 * dr + 3, cs])
            term = w_l * l3[dr:dr + ng] + w_c * a3[dr:dr + ng] + w_r * r3[dr:dr + ng]
            conv = term if conv is None else conv + term
        hid = (_silu(conv + db_ref[:, cs]) * b_all[s].reshape(ng, GRID_W, sub)).astype(BF16).reshape(tm, sub)
        part = jnp.dot(hid, wd_ref[cs, :], preferred_element_type=F32)
        acc = part if acc is None else acc + part
    o_ref[...] += acc

    @pl.when(f == pl.num_programs(1) - 1)
    def _():
        x2 = x1_ref[...] + g2_ref[0:1, :] * o_ref[...]
        o_ref[...] = x2 * lax.rsqrt(jnp.mean(x2 * x2, axis=-1, keepdims=True) + EPS) * fg_ref[...]


def _ffn(h2, x1, mod, w_up, w_down, dw9, db, fg, *, tm, tf, sub):
    L, d = x1.shape
    nt = L // tm
    nf = D_FF // tf
    rg = tm // GRID_W
    lastg = L // GRID_W - 1
    return pl.pallas_call(
        functools.partial(_ffn_kernel, sub=sub),
        grid=(nt, nf),
        in_specs=[pl.BlockSpec((tm, d), lambda i, f: (i, 0)),
                  pl.BlockSpec((GRID_W, d), lambda i, f: (jnp.maximum(i * rg - 1, 0), 0)),
                  pl.BlockSpec((GRID_W, d), lambda i, f: (jnp.minimum((i + 1) * rg, lastg), 0)),
                  pl.BlockSpec((tm, d), lambda i, f: (i, 0)),
                  pl.BlockSpec((d, tf), lambda i, f: (0, f)),
                  pl.BlockSpec((d, tf), lambda i, f: (0, nf + f)),
                  pl.BlockSpec((tf, d), lambda i, f: (f, 0)),
                  pl.BlockSpec((9, tf), lambda i, f: (0, f)),
                  pl.BlockSpec((1, tf), lambda i, f: (0, f)),
                  pl.BlockSpec((8, d), lambda i, f: (0, 5)),
                  pl.BlockSpec((1, d), lambda i, f: (0, 0))],
        out_specs=pl.BlockSpec((tm, d), lambda i, f: (i, 0)),
        out_shape=jax.ShapeDtypeStruct((L, d), F32),
        scratch_shapes=[pltpu.VMEM((tm + 2 * GRID_W, d), BF16)],
        compiler_params=_params("arbitrary", "arbitrary"),
        name="conv_ffn",
    )(h2, h2, h2, x1, w_up, w_up, w_down, dw9, db, mod, fg)


def kernel(x, c, ctx, c_ctx, w_mod, b_mod, norm1_g, w_in, lb_raw, a_norm_g, sconv_w, w_pa, w_pb, w_o,
           norm2_g, w_up, ffn_dw, ffn_db, w_down, final_g):
    assert x.shape[0] == 1 and w_in.shape[0] == 1, "single-sequence, depth-1 layer"
    d = D_MODEL
    x2d = x[0]
    ctx2d = ctx[0]
    t_lat = _tiles(x2d.shape[0])
    t_ctx = _tiles(ctx2d.shape[0])

    cc = jnp.zeros((16, d), F32).at[0].set(c[0]).at[1].set(c_ctx)
    mod = _modulation(cc, w_mod[0], b_mod[0][None, :], t_lat["mod_tn"])
    lb = _lower_bounds(lb_raw.reshape(lb_raw.shape[0], 2 * A_WIDTH))

    w_in16 = w_in[0].astype(BF16)
    g1n = norm1_g[0][None, :]
    (rec_c,) = _inproj(ctx2d, mod, g1n, w_in16, lb, mod_row=1, full=False, tm=t_ctx["mix_tm"])
    s_zero = jnp.zeros((2, N_HEADS, HEAD_DIM, HEAD_DIM), F32)
    _, _, s_ctx = _scan(rec_c, s_zero, t_ctx["scan_chunks"])
    rec, rest = _inproj(x2d, mod, g1n, w_in16, lb, mod_row=0, full=True, tm=t_lat["inproj_tm"])
    o_f, o_b, _ = _scan(rec, s_ctx, t_lat["scan_chunks"])
    x1, h2 = _mixout(o_f, o_b, rest, x2d, mod, norm2_g[0][None, :], a_norm_g[0][None, :], sconv_w[0],
                     w_pa[0].astype(BF16), w_pb[0].astype(BF16), w_o[0].astype(BF16), tm=t_lat["mix_tm"])
    out = _ffn(h2, x1, mod, w_up[0].astype(BF16), w_down[0].astype(BF16), ffn_dw[0].reshape(9, D_FF),
               ffn_db[0][None, :], final_g[None, :], tm=t_lat["ffn_tm"], tf=t_lat["ffn_tf"], sub=t_lat["ffn_sub"])
    return out[None]
```

```python
import functools
import math

import jax
import jax.numpy as jnp
from jax import lax
from jax.experimental import pallas as pl
from jax.experimental.pallas import tpu as pltpu

D_MODEL = 2048
N_HEADS = 8
HEAD_DIM = 128
A_WIDTH = N_HEADS * HEAD_DIM
B_WIDTH = 1024
D_FF = 5632
CHUNK = 64
GRID_W = 64
EPS = 1e-6
SEG = 1024
N_SEG_REC = 4
SEG_H = 7
REST_GATE_A, REST_GATE_B, REST_G, REST_B, REST_U, N_REST = 0, 2, 4, 5, 6, 7
V7X_VMEM_BYTES = 64 * 1024 * 1024
VMEM_LIMIT = V7X_VMEM_BYTES - 8 * 1024 * 1024
BF16_ROWS = 16
LOG2E = math.log2(math.e)
BF16 = jnp.bfloat16
F32 = jnp.float32


def _tiles(L):
    return dict(inproj_tm=min(L, 1024), mix_tm=min(L, 256), ctx_tm=min(L, 256), ffn_tm=min(L, 512), ffn_tf=512, ffn_sub=256,
                scan_chunks=min(L // CHUNK, 4), mod_tn=512)


def _sigmoid(z):
    return 0.5 * jnp.tanh(0.5 * z) + 0.5


def _silu(z):
    return z * _sigmoid(z)


def _const_spec(shape, index=None):
    idx = index if index is not None else (0,) * len(shape)
    return pl.BlockSpec(shape, lambda *_: idx, pipeline_mode=pl.Buffered(1))


def _params(*semantics):
    return pltpu.CompilerParams(dimension_semantics=semantics, vmem_limit_bytes=VMEM_LIMIT)


def _mod_kernel(cc_ref, w_ref, b_ref, o_ref):
    s = _silu(cc_ref[...]).astype(BF16)
    o_ref[...] = jnp.dot(s, w_ref[...].astype(BF16), preferred_element_type=F32) + b_ref[...]


def _modulation(cc, w_mod, b_mod, tn):
    rows, d = cc.shape
    n = w_mod.shape[1]
    return pl.pallas_call(
        _mod_kernel,
        grid=(n // tn,),
        in_specs=[pl.BlockSpec((rows, d), lambda j: (0, 0)),
                  pl.BlockSpec((d, tn), lambda j: (0, j)),
                  pl.BlockSpec((1, tn), lambda j: (0, j))],
        out_specs=pl.BlockSpec((rows, tn), lambda j: (0, j)),
        out_shape=jax.ShapeDtypeStruct((rows, n), F32),
        compiler_params=_params("arbitrary"),
        name="modulation",
    )(cc, w_mod, b_mod)


def _lb_kernel(raw_ref, o_ref):
    raw = raw_ref[...]
    e = jnp.exp(raw - jnp.max(raw, axis=0, keepdims=True))
    o_ref[...] = e[0:1, :] / jnp.sum(e, axis=0, keepdims=True)


def _lower_bounds(lb_raw2d):
    return pl.pallas_call(
        _lb_kernel,
        out_shape=jax.ShapeDtypeStruct((1, lb_raw2d.shape[1]), F32),
        name="lower_bounds",
    )(lb_raw2d)


def _inproj_kernel(*refs, mod_row, full):
    if full:
        x_ref, sh_ref, sc_ref, g_ref, w_ref, wh_ref, lb_ref, rec_ref, rest_ref, hx_ref = refs
    else:
        x_ref, sh_ref, sc_ref, g_ref, w_ref, lb_ref, rec_ref, hx_ref = refs
    j = pl.program_id(1)

    @pl.when(j == 0)
    def _():
        x = x_ref[...]
        y = x * lax.rsqrt(jnp.mean(x * x, axis=-1, keepdims=True) + EPS) * g_ref[...]
        sh = sh_ref[mod_row:mod_row + 1, :]
        sc = sc_ref[mod_row:mod_row + 1, :]
        hx_ref[...] = (y * (1.0 + sc) + sh).astype(BF16)

    tm = hx_ref.shape[0]
    halves = [slice(r * (tm // 2), (r + 1) * (tm // 2)) for r in range(2)]

    def z(rows, wr=w_ref):
        return jnp.dot(hx_ref[rows, :], wr[...], preferred_element_type=F32)

    @pl.when(j == 0)
    def _():
        for rows in halves:
            rec_ref[rows, :] = _silu(z(rows))

    @pl.when((j == 1) | (j == 2))
    def _():
        lb = jnp.where(j == 1, lb_ref[:, 0:SEG], lb_ref[:, SEG:2 * SEG])
        for rows in halves:
            rec_ref[rows, :] = jnp.log(lb + (1.0 - lb) * _sigmoid(z(rows)))

    @pl.when(j == 3)
    def _():
        for rows in halves:
            rec_ref[rows, :] = z(rows)

    if full:
        @pl.when(j == 4)
        def _():
            for rows in halves:
                rest_ref[rows, :] = _silu(z(rows)).astype(BF16)

        @pl.when(j == 5)
        def _():
            for rows in halves:
                rest_ref[rows, :] = z(rows).astype(BF16)

        @pl.when(j == 6)
        def _():
            for rows in halves:
                rest_ref[rows, :] = (z(rows) * z(rows, wh_ref)).astype(BF16)

        @pl.when(j >= 7)
        def _():
            for rows in halves:
                rest_ref[rows, :] = _sigmoid(z(rows)).astype(BF16)


def _inproj(x2d, mod, g, w, lb, *, mod_row, full, tm):
    L, d = x2d.shape
    n_steps = 11 if full else N_SEG_REC
    w_seg = lambda i, j: (0, jnp.where(j >= SEG_H, j + 1, j))
    rest_blk = lambda i, j: (i, jnp.where(j < REST_G, REST_G, jnp.where(j <= REST_U, j, j - 7)))
    in_specs = [pl.BlockSpec((tm, d), lambda i, j: (i, 0)),
                pl.BlockSpec((8, d), lambda i, j: (0, 0)),
                pl.BlockSpec((8, d), lambda i, j: (0, 1)),
                pl.BlockSpec((1, d), lambda i, j: (0, 0)),
                pl.BlockSpec((d, SEG), w_seg)]
    args = [x2d, mod, mod, g, w]
    out_specs = [pl.BlockSpec((tm, SEG), lambda i, j: (i, jnp.minimum(j, N_SEG_REC - 1)))]
    out_shape = [jax.ShapeDtypeStruct((L, N_SEG_REC * SEG), F32)]
    if full:
        in_specs.append(_const_spec((d, SEG), (0, SEG_H)))
        args.append(w)
        out_specs.append(pl.BlockSpec((tm, SEG), rest_blk))
        out_shape.append(jax.ShapeDtypeStruct((L, N_REST * SEG), BF16))
    in_specs.append(pl.BlockSpec((1, 2 * SEG), lambda i, j: (0, 0)))
    args.append(lb)
    return pl.pallas_call(
        functools.partial(_inproj_kernel, mod_row=mod_row, full=full),
        grid=(L // tm, n_steps),
        in_specs=in_specs,
        out_specs=out_specs,
        out_shape=out_shape,
        scratch_shapes=[pltpu.VMEM((tm, d), BF16)],
        compiler_params=_params("arbitrary", "arbitrary"),
        name="inproj_full" if full else "inproj_rec",
    )(*args)


LEVEL_HALVES = (32, 16, 8, 4, 2, 1)


def _scan_consts():
    t = jnp.arange(CHUNK)
    tril = (t[:, None] >= t[None, :])
    cums = jnp.stack([tril, tril.T]).astype(BF16)
    masks = []
    for half in LEVEL_HALVES:
        blk = 2 * half
        same = (t[:, None] // blk) == (t[None, :] // blk)
        fwd = same & ((t[:, None] % blk) >= half) & ((t[None, :] % blk) < half)
        masks.append(jnp.stack([fwd, fwd.T]))
    masks = jnp.stack(masks, axis=1).astype(F32)
    return cums, masks


def _block_reference_rows(b, half, ridx):
    blk = 2 * half
    n = CHUNK // blk
    width = b.shape[1]
    if blk >= 8:
        parts = [jnp.broadcast_to(b[i * blk + ridx:i * blk + ridx + 1, :], (blk, width)) for i in range(n)]
        return parts[0] if n == 1 else jnp.concatenate(parts, axis=0)
    pos = lax.broadcasted_iota(jnp.int32, b.shape, 0) & (blk - 1)
    r = b
    for p in range(blk):
        delta = ridx - p
        if delta != 0:
            r = jnp.where(pos == p, pltpu.roll(b, (-delta) % CHUNK, axis=0), r)
    return r


def _neg_abs(x):
    bits = lax.bitcast_convert_type(x, jnp.uint32) | jnp.uint32(0x80000000)
    return lax.bitcast_convert_type(bits, F32)


def _query_key_rows(rev, half, q, k):
    blk = 2 * half
    if half >= 8:
        parts = []
        for i in range(CHUNK // blk):
            lo, mid, hi = i * blk, i * blk + half, (i + 1) * blk
            parts += [q[lo:mid], k[mid:hi]] if rev else [k[lo:mid], q[mid:hi]]
        return jnp.concatenate(parts, axis=0)
    pos = lax.broadcasted_iota(jnp.int32, q.shape, 0) & (blk - 1)
    return jnp.where((pos < half) if rev else (pos >= half), q, k)


def _scan_cumulative(d, g_ref, rows, cum, b_ref):
    g = g_ref[rows, :]
    g_hi = g.astype(BF16)
    r1 = g - g_hi.astype(F32)
    g_mid = r1.astype(BF16)
    g_lo = (r1 - g_mid.astype(F32)).astype(BF16)
    b_ref[d] = (jnp.dot(cum, g_hi, preferred_element_type=F32) + jnp.dot(cum, g_mid, preferred_element_type=F32)
                + jnp.dot(cum, g_lo, preferred_element_type=F32)) * LOG2E


def _scan_scores(d, h, q_ref, g_ref, v_ref, rows, masks_ref, b_ref, pos):
    rev = d == 1
    sl = slice(h * HEAD_DIM, (h + 1) * HEAD_DIM)
    end = 0 if rev else CHUNK - 1
    q = q_ref[rows, sl]
    v = v_ref[rows, sl]
    f = jnp.exp(g_ref[rows, sl])
    k = 1.0 - f
    qf = q * f
    b = b_ref[d, :, sl]
    b_end = b_ref[d, end:end + 1, sl]
    scores = None
    for lvl, half in enumerate(LEVEL_HALVES):
        if half > 2:
            r = _block_reference_rows(b, half, half if rev else half - 1)
            xl = _query_key_rows(rev, half, q, k) * jnp.exp2(_neg_abs(b - r))
        elif half == 2:
            f_prev = pltpu.roll(f, 1, axis=0)
            f_next = pltpu.roll(f, CHUNK - 1, axis=0)
            if rev:
                xl = jnp.where(pos["q4"], qf * jnp.where(pos["first4"], f_next, 1.0), k * jnp.where(pos["last4"], f_prev, 1.0))
            else:
                xl = jnp.where(pos["q4"], qf * jnp.where(pos["last4"], f_prev, 1.0), k * jnp.where(pos["first4"], f_next, 1.0))
        else:
            xl = jnp.where(pos["q2"], qf, k)
        xl = xl.astype(BF16)
        a = lax.dot_general(xl, xl, (((1,), (1,)), ((), ())), preferred_element_type=F32) * masks_ref[d, lvl]
        scores = a if scores is None else scores + a
    qb = (q * jnp.exp2(b)).astype(BF16)
    kdec = (k * jnp.exp2(b_end - b)).astype(BF16)
    own = jnp.sum(q * k, axis=-1, keepdims=True) * v
    return scores.astype(BF16), qb, kdec, v.astype(BF16), own, jnp.exp2(b_end)


def _scan_state(d, h, staged, rows, s_ref, o_ref):
    p16, qb, kdec, v16, own, decay = staged
    sl = slice(h * HEAD_DIM, (h + 1) * HEAD_DIM)
    st = s_ref[d, h]
    o_inter = lax.dot_general(qb, st.astype(BF16), (((1,), (1,)), ((), ())), preferred_element_type=F32)
    o_intra = jnp.dot(p16, v16, preferred_element_type=F32)
    o_ref[rows, sl] = (o_inter + o_intra + own).astype(o_ref.dtype)
    upd = lax.dot_general(v16, kdec, (((0,), (0,)), ((), ())), preferred_element_type=F32)
    s_ref[d, h] = st * decay + upd


SCAN_LAG = 2


def _scan_kernel(qf_ref, gf_ref, vf_ref, qb_ref, gb_ref, vb_ref, s0_ref, cum_ref, masks_ref,
                 of_ref, ob_ref, sfin_ref, s_ref, b_ref, *, n_sub):
    c = pl.program_id(0)

    @pl.when(c == 0)
    def _():
        s_ref[...] = s0_ref[...]

    def body(cc, carry):
        rows = (pl.ds(pl.multiple_of(cc * CHUNK, CHUNK), CHUNK),
                pl.ds(pl.multiple_of((n_sub - 1 - cc) * CHUNK, CHUNK), CHUNK))
        ins = ((qf_ref, gf_ref, vf_ref), (qb_ref, gb_ref, vb_ref))
        outs = (of_ref, ob_ref)
        for d in range(2):
            _scan_cumulative(d, ins[d][1], rows[d], cum_ref[d], b_ref)
        r4 = lax.broadcasted_iota(jnp.int32, (CHUNK, HEAD_DIM), 0) & 3
        pos = [dict(q4=(r4 < 2) if d else (r4 >= 2), first4=r4 == 0, last4=r4 == 3,
                    q2=((r4 & 1) == 0) if d else ((r4 & 1) == 1)) for d in range(2)]
        chains = [(d, h) for h in range(N_HEADS) for d in range(2)]
        staged = {}
        for n in range(len(chains) + SCAN_LAG):
            if n < len(chains):
                d, h = chains[n]
                staged[n] = _scan_scores(d, h, *ins[d], rows[d], masks_ref, b_ref, pos[d])
            if n >= SCAN_LAG:
                d, h = chains[n - SCAN_LAG]
                _scan_state(d, h, staged.pop(n - SCAN_LAG), rows[d], s_ref, outs[d])
        return carry

    lax.fori_loop(0, n_sub, body, 0)

    @pl.when(c == pl.num_programs(0) - 1)
    def _():
        sfin_ref[...] = s_ref[...]


def _scan(rec, s0, n_sub):
    L = rec.shape[0]
    ns = L // (CHUNK * n_sub)
    cums, masks = _scan_consts()
    blk = (CHUNK * n_sub, SEG)
    state_spec = pl.BlockSpec(s0.shape, lambda c: (0, 0, 0, 0))
    return pl.pallas_call(
        functools.partial(_scan_kernel, n_sub=n_sub),
        grid=(ns,),
        in_specs=[pl.BlockSpec(blk, lambda c: (c, 0)),
                  pl.BlockSpec(blk, lambda c: (c, 1)),
                  pl.BlockSpec(blk, lambda c: (c, 3)),
                  pl.BlockSpec(blk, lambda c: (ns - 1 - c, 0)),
                  pl.BlockSpec(blk, lambda c: (ns - 1 - c, 2)),
                  pl.BlockSpec(blk, lambda c: (ns - 1 - c, 3)),
                  state_spec,
                  pl.BlockSpec(cums.shape, lambda c: (0, 0, 0)),
                  pl.BlockSpec(masks.shape, lambda c: (0, 0, 0, 0))],
        out_specs=[pl.BlockSpec(blk, lambda c: (c, 0)),
                   pl.BlockSpec(blk, lambda c: (ns - 1 - c, 0)),
                   state_spec],
        out_shape=[jax.ShapeDtypeStruct((L, A_WIDTH), BF16),
                   jax.ShapeDtypeStruct((L, A_WIDTH), BF16),
                   jax.ShapeDtypeStruct(s0.shape, F32)],
        scratch_shapes=[pltpu.VMEM(s0.shape, F32), pltpu.VMEM((2, CHUNK, A_WIDTH), F32)],
        compiler_params=_params("arbitrary"),
        name=f"scan_{ns}",
    )(rec, rec, rec, rec, rec, rec, s0, cums, masks)


def _mixout_kernel(of_ref, ob_ref, sg_ref, sb_ref, u_ref, up_ref, un_ref, ga_ref, gb_ref, x_ref,
                   g1_ref, sh2_ref, sc2_ref, n2_ref, ag_ref, cw_ref, wpa_ref, wpb_ref, wo_ref,
                   x1_ref, h2_ref, ya_ref):
    i = pl.program_id(0)
    tm = x_ref.shape[0]
    for h in range(N_HEADS):
        sl = slice(h * HEAD_DIM, (h + 1) * HEAD_DIM)
        o = of_ref[:, sl].astype(F32) + ob_ref[:, sl].astype(F32)
        o = o * lax.rsqrt(jnp.mean(o * o, axis=-1, keepdims=True) + EPS) * ag_ref[...]
        ya_ref[:, sl] = (o * sg_ref[:, sl].astype(F32)).astype(BF16)
    u = u_ref[...].astype(F32)
    rows = lax.broadcasted_iota(jnp.int32, u.shape, 0)
    prev_row = jnp.where(i > 0, up_ref[...].astype(F32)[BF16_ROWS - 1:BF16_ROWS, :], 0.0)
    next_row = jnp.where(i < pl.num_programs(0) - 1, un_ref[...].astype(F32)[0:1, :], 0.0)
    u_prev = jnp.where(rows == 0, prev_row, pltpu.roll(u, 1, axis=0))
    u_next = jnp.where(rows == tm - 1, next_row, pltpu.roll(u, tm - 1, axis=0))
    yb = sb_ref[...].astype(F32) * (cw_ref[0:1, :] * u_prev + cw_ref[1:2, :] * u + cw_ref[2:3, :] * u_next)
    pa = jnp.dot(ya_ref[...], wpa_ref[...], preferred_element_type=F32)
    pb = jnp.dot(yb.astype(BF16), wpb_ref[...], preferred_element_type=F32)
    merged = (ga_ref[...].astype(F32) * pa + gb_ref[...].astype(F32) * pb).astype(BF16)
    out = jnp.dot(merged, wo_ref[...], preferred_element_type=F32)
    x1 = x_ref[...] + g1_ref[0:1, :] * out
    x1_ref[...] = x1
    y = x1 * lax.rsqrt(jnp.mean(x1 * x1, axis=-1, keepdims=True) + EPS) * n2_ref[...]
    h2_ref[...] = (y * (1.0 + sc2_ref[0:1, :]) + sh2_ref[0:1, :]).astype(BF16)


def _mixout(o_f, o_b, rest, x2d, mod, n2g, ag, cw, wpa, wpb, wo, *, tm):
    L, d = x2d.shape
    nt = L // tm
    rt = tm // BF16_ROWS
    last_t = L // BF16_ROWS - 1
    row = lambda i: (i, 0)
    return pl.pallas_call(
        _mixout_kernel,
        grid=(nt,),
        in_specs=[pl.BlockSpec((tm, A_WIDTH), row),
                  pl.BlockSpec((tm, A_WIDTH), row),
                  pl.BlockSpec((tm, SEG), lambda i: (i, REST_G)),
                  pl.BlockSpec((tm, SEG), lambda i: (i, REST_B)),
                  pl.BlockSpec((tm, SEG), lambda i: (i, REST_U)),
                  pl.BlockSpec((BF16_ROWS, SEG), lambda i: (jnp.maximum(i * rt - 1, 0), REST_U)),
                  pl.BlockSpec((BF16_ROWS, SEG), lambda i: (jnp.minimum((i + 1) * rt, last_t), REST_U)),
                  pl.BlockSpec((tm, d), lambda i: (i, REST_GATE_A // 2)),
                  pl.BlockSpec((tm, d), lambda i: (i, REST_GATE_B // 2)),
                  pl.BlockSpec((tm, d), row),
                  pl.BlockSpec((8, d), lambda i: (0, 2)),
                  pl.BlockSpec((8, d), lambda i: (0, 3)),
                  pl.BlockSpec((8, d), lambda i: (0, 4)),
                  pl.BlockSpec((1, d), lambda i: (0, 0)),
                  pl.BlockSpec((1, HEAD_DIM), lambda i: (0, 0)),
                  pl.BlockSpec(cw.shape, lambda i: (0, 0)),
                  _const_spec(wpa.shape), _const_spec(wpb.shape), _const_spec(wo.shape)],
        out_specs=[pl.BlockSpec((tm, d), row), pl.BlockSpec((tm, d), row)],
        out_shape=[jax.ShapeDtypeStruct((L, d), F32), jax.ShapeDtypeStruct((L, d), BF16)],
        scratch_shapes=[pltpu.VMEM((tm, A_WIDTH), BF16)],
        compiler_params=_params("arbitrary"),
        name="mixer_out",
    )(o_f, o_b, rest, rest, rest, rest, rest, rest, rest, x2d, mod, mod, mod, n2g, ag, cw, wpa, wpb, wo)


def _ffn_kernel(h_ref, hb_ref, x1_ref, wa_ref, wb_ref, wd_ref, dw_ref, db_ref, g2_ref, fg_ref,
                o_ref, hx_ref, edge_ref, *, sub):
    i = pl.program_id(0)
    f = pl.program_id(1)
    tm = h_ref.shape[0]
    tf = wa_ref.shape[1]
    ext = tm + 2 * GRID_W

    @pl.when(f == 0)
    def _():
        hx_ref[0:tm, :] = h_ref[...]
        hx_ref[tm:tm + GRID_W, :] = jnp.where(i < pl.num_programs(0) - 1, hb_ref[...], jnp.zeros(hb_ref.shape, BF16))
        o_ref[...] = jnp.zeros(o_ref.shape, F32)

    @pl.when((f == 0) & (i == 0))
    def _():
        edge_ref[...] = jnp.zeros(edge_ref.shape, F32)

    ng = tm // GRID_W
    n_sub = tf // sub
    a_all = [jnp.dot(hx_ref[...], wa_ref[:, s * sub:(s + 1) * sub], preferred_element_type=F32) for s in range(n_sub)]
    b_all = [jnp.dot(hx_ref[0:tm, :], wb_ref[:, s * sub:(s + 1) * sub], preferred_element_type=F32)
             for s in range(n_sub)]
    colg = lax.broadcasted_iota(jnp.int32, (GRID_W, sub), 0)
    acc = None
    for s in range(n_sub):
        cs = slice(s * sub, (s + 1) * sub)
        a = jnp.concatenate([edge_ref[f, :, cs], a_all[s]], axis=0)
        edge_ref[f, :, cs] = a_all[s][tm - GRID_W:tm]
        a3 = a.reshape(ng + 2, GRID_W, sub)
        l3 = pltpu.roll(a, 1, axis=0).reshape(ng + 2, GRID_W, sub)
        r3 = pltpu.roll(a, ext - 1, axis=0).reshape(ng + 2, GRID_W, sub)
        conv = None
        for dr in range(3):
            w_l = jnp.where(colg == 0, 0.0, dw_ref[3 * dr:3 * dr + 1, cs])
            w_c = dw_ref[3 * dr + 1:3 * dr + 2, cs]
            w_r = jnp.where(colg == GRID_W - 1, 0.0, dw_ref[3 * dr + 2:3 * dr + 3, cs])
            term = w_l * l3[dr:dr + ng] + w_c * a3[dr:dr + ng] + w_r * r3[dr:dr + ng]
            conv = term if conv is None else conv + term
        hid = (_silu(conv + db_ref[:, cs]) * b_all[s].reshape(ng, GRID_W, sub)).astype(BF16).reshape(tm, sub)
        part = jnp.dot(hid, wd_ref[cs, :], preferred_element_type=F32)
        acc = part if acc is None else acc + part
    o_ref[...] += acc

    @pl.when(f == pl.num_programs(1) - 1)
    def _():
        x2 = x1_ref[...] + g2_ref[0:1, :] * o_ref[...]
        o_ref[...] = x2 * lax.rsqrt(jnp.mean(x2 * x2, axis=-1, keepdims=True) + EPS) * fg_ref[...]


def _ffn(h2, x1, mod, w_up, w_down, dw9, db, fg, *, tm, tf, sub):
    L, d = x1.shape
    nt = L // tm
    nf = D_FF // tf
    rg = tm // GRID_W
    lastg = L // GRID_W - 1
    return pl.pallas_call(
        functools.partial(_ffn_kernel, sub=sub),
        grid=(nt, nf),
        in_specs=[pl.BlockSpec((tm, d), lambda i, f: (i, 0)),
                  pl.BlockSpec((GRID_W, d), lambda i, f: (jnp.minimum((i + 1) * rg, lastg), 0)),
                  pl.BlockSpec((tm, d), lambda i, f: (i, 0)),
                  pl.BlockSpec((d, tf), lambda i, f: (0, f)),
                  pl.BlockSpec((d, tf), lambda i, f: (0, nf + f)),
                  pl.BlockSpec((tf, d), lambda i, f: (f, 0)),
                  pl.BlockSpec((9, tf), lambda i, f: (0, f)),
                  pl.BlockSpec((1, tf), lambda i, f: (0, f)),
                  pl.BlockSpec((8, d), lambda i, f: (0, 5)),
                  pl.BlockSpec((1, d), lambda i, f: (0, 0))],
        out_specs=pl.BlockSpec((tm, d), lambda i, f: (i, 0)),
        out_shape=jax.ShapeDtypeStruct((L, d), F32),
        scratch_shapes=[pltpu.VMEM((tm + GRID_W, d), BF16),
                        pltpu.VMEM((nf, GRID_W, tf), F32)],
        compiler_params=_params("arbitrary", "arbitrary"),
        name="conv_ffn",
    )(h2, h2, x1, w_up, w_up, w_down, dw9, db, mod, fg)


def kernel(x, c, ctx, c_ctx, w_mod, b_mod, norm1_g, w_in, lb_raw, a_norm_g, sconv_w, w_pa, w_pb, w_o,
           norm2_g, w_up, ffn_dw, ffn_db, w_down, final_g):
    assert x.shape[0] == 1 and w_in.shape[0] == 1, "single-sequence, depth-1 layer"
    d = D_MODEL
    x2d = x[0]
    ctx2d = ctx[0]
    t_lat = _tiles(x2d.shape[0])
    t_ctx = _tiles(ctx2d.shape[0])

    cc = jnp.zeros((16, d), F32).at[0].set(c[0]).at[1].set(c_ctx)
    mod = _modulation(cc, w_mod[0], b_mod[0][None, :], t_lat["mod_tn"])
    lb = _lower_bounds(lb_raw.reshape(lb_raw.shape[0], 2 * A_WIDTH))

    w_in16 = w_in[0].astype(BF16)
    g1n = norm1_g[0][None, :]
    (rec_c,) = _inproj(ctx2d, mod, g1n, w_in16, lb, mod_row=1, full=False, tm=t_ctx["ctx_tm"])
    s_zero = jnp.zeros((2, N_HEADS, HEAD_DIM, HEAD_DIM), F32)
    _, _, s_ctx = _scan(rec_c, s_zero, t_ctx["scan_chunks"])
    rec, rest = _inproj(x2d, mod, g1n, w_in16, lb, mod_row=0, full=True, tm=t_lat["inproj_tm"])
    o_f, o_b, _ = _scan(rec, s_ctx, t_lat["scan_chunks"])
    x1, h2 = _mixout(o_f, o_b, rest, x2d, mod, norm2_g[0][None, :], a_norm_g[0][None, :], sconv_w[0],
                     w_pa[0].astype(BF16), w_pb[0].astype(BF16), w_o[0].astype(BF16), tm=t_lat["mix_tm"])
    out = _ffn(h2, x1, mod, w_up[0].astype(BF16), w_down[0].astype(BF16), ffn_dw[0].reshape(9, D_FF),
               ffn_db[0][None, :], final_g[None, :], tm=t_lat["ffn_tm"], tf=t_lat["ffn_tf"], sub=t_lat["ffn_sub"])
    return out[None]
```

```python
import functools
import math

import jax
import jax.numpy as jnp
from jax import lax
from jax.experimental import pallas as pl
from jax.experimental.pallas import tpu as pltpu

D_MODEL = 2048
N_HEADS = 8
HEAD_DIM = 128
A_WIDTH = N_HEADS * HEAD_DIM
B_WIDTH = 1024
D_FF = 5632
CHUNK = 64
GRID_W = 64
EPS = 1e-6
SEG = 1024
N_SEG_REC = 4
SEG_H = 7
REST_GATE_A, REST_GATE_B, REST_G, REST_B, REST_U, N_REST = 0, 2, 4, 5, 6, 7
V7X_VMEM_BYTES = 64 * 1024 * 1024
VMEM_LIMIT = V7X_VMEM_BYTES - 8 * 1024 * 1024
BF16_ROWS = 16
LOG2E = math.log2(math.e)
BF16 = jnp.bfloat16
F32 = jnp.float32


def _tiles(L):
    return dict(inproj_tm=min(L, 1024), mix_tm=min(L, 256), ctx_tm=min(L, 256), ffn_tm=min(L, 512), ffn_tf=512, ffn_sub=256,
                scan_chunks=min(L // CHUNK, 8), mod_tn=512)


def _sigmoid(z):
    return 0.5 * jnp.tanh(0.5 * z) + 0.5


def _silu(z):
    return z * _sigmoid(z)


def _const_spec(shape, index=None):
    idx = index if index is not None else (0,) * len(shape)
    return pl.BlockSpec(shape, lambda *_: idx, pipeline_mode=pl.Buffered(1))


def _params(*semantics):
    return pltpu.CompilerParams(dimension_semantics=semantics, vmem_limit_bytes=VMEM_LIMIT)


def _mod_kernel(cc_ref, w_ref, b_ref, o_ref):
    s = _silu(cc_ref[...]).astype(BF16)
    o_ref[...] = jnp.dot(s, w_ref[...].astype(BF16), preferred_element_type=F32) + b_ref[...]


def _modulation(cc, w_mod, b_mod, tn):
    rows, d = cc.shape
    n = w_mod.shape[1]
    return pl.pallas_call(
        _mod_kernel,
        grid=(n // tn,),
        in_specs=[pl.BlockSpec((rows, d), lambda j: (0, 0)),
                  pl.BlockSpec((d, tn), lambda j: (0, j)),
                  pl.BlockSpec((1, tn), lambda j: (0, j))],
        out_specs=pl.BlockSpec((rows, tn), lambda j: (0, j)),
        out_shape=jax.ShapeDtypeStruct((rows, n), F32),
        compiler_params=_params("arbitrary"),
        name="modulation",
    )(cc, w_mod, b_mod)


def _lb_kernel(raw_ref, o_ref):
    raw = raw_ref[...]
    e = jnp.exp(raw - jnp.max(raw, axis=0, keepdims=True))
    o_ref[...] = e[0:1, :] / jnp.sum(e, axis=0, keepdims=True)


def _lower_bounds(lb_raw2d):
    return pl.pallas_call(
        _lb_kernel,
        out_shape=jax.ShapeDtypeStruct((1, lb_raw2d.shape[1]), F32),
        name="lower_bounds",
    )(lb_raw2d)


def _inproj_kernel(*refs, mod_row, full):
    if full:
        x_ref, sh_ref, sc_ref, g_ref, w_ref, wh_ref, lb_ref, rec_ref, rest_ref, hx_ref = refs
    else:
        x_ref, sh_ref, sc_ref, g_ref, w_ref, lb_ref, rec_ref, hx_ref = refs
    j = pl.program_id(1)

    @pl.when(j == 0)
    def _():
        x = x_ref[...]
        y = x * lax.rsqrt(jnp.mean(x * x, axis=-1, keepdims=True) + EPS) * g_ref[...]
        sh = sh_ref[mod_row:mod_row + 1, :]
        sc = sc_ref[mod_row:mod_row + 1, :]
        hx_ref[...] = (y * (1.0 + sc) + sh).astype(BF16)

    tm = hx_ref.shape[0]
    halves = [slice(r * (tm // 4), (r + 1) * (tm // 4)) for r in range(4)]

    def z(rows, wr=w_ref):
        return jnp.dot(hx_ref[rows, :], wr[...], preferred_element_type=F32)

    @pl.when(j == 0)
    def _():
        for rows in halves:
            rec_ref[rows, :] = _silu(z(rows))

    @pl.when((j == 1) | (j == 2))
    def _():
        lb = jnp.where(j == 1, lb_ref[:, 0:SEG], lb_ref[:, SEG:2 * SEG])
        for rows in halves:
            rec_ref[rows, :] = jnp.log(lb + (1.0 - lb) * _sigmoid(z(rows)))

    @pl.when(j == 3)
    def _():
        for rows in halves:
            rec_ref[rows, :] = z(rows)

    if full:
        @pl.when(j == 4)
        def _():
            for rows in halves:
                rest_ref[rows, :] = _silu(z(rows)).astype(BF16)

        @pl.when(j == 5)
        def _():
            for rows in halves:
                rest_ref[rows, :] = z(rows).astype(BF16)

        @pl.when(j == 6)
        def _():
            for rows in halves:
                rest_ref[rows, :] = (z(rows) * z(rows, wh_ref)).astype(BF16)

        @pl.when(j >= 7)
        def _():
            for rows in halves:
                rest_ref[rows, :] = _sigmoid(z(rows)).astype(BF16)


def _inproj(x2d, mod, g, w, lb, *, mod_row, full, tm):
    L, d = x2d.shape
    n_steps = 11 if full else N_SEG_REC
    w_seg = lambda i, j: (0, jnp.where(j >= SEG_H, j + 1, j))
    rest_blk = lambda i, j: (i, jnp.where(j < REST_G, REST_G, jnp.where(j <= REST_U, j, j - 7)))
    in_specs = [pl.BlockSpec((tm, d), lambda i, j: (i, 0)),
                pl.BlockSpec((8, d), lambda i, j: (0, 0)),
                pl.BlockSpec((8, d), lambda i, j: (0, 1)),
                pl.BlockSpec((1, d), lambda i, j: (0, 0)),
                pl.BlockSpec((d, SEG), w_seg)]
    args = [x2d, mod, mod, g, w]
    out_specs = [pl.BlockSpec((tm, SEG), lambda i, j: (i, jnp.minimum(j, N_SEG_REC - 1)))]
    out_shape = [jax.ShapeDtypeStruct((L, N_SEG_REC * SEG), F32)]
    if full:
        in_specs.append(_const_spec((d, SEG), (0, SEG_H)))
        args.append(w)
        out_specs.append(pl.BlockSpec((tm, SEG), rest_blk))
        out_shape.append(jax.ShapeDtypeStruct((L, N_REST * SEG), BF16))
    in_specs.append(pl.BlockSpec((1, 2 * SEG), lambda i, j: (0, 0)))
    args.append(lb)
    return pl.pallas_call(
        functools.partial(_inproj_kernel, mod_row=mod_row, full=full),
        grid=(L // tm, n_steps),
        in_specs=in_specs,
        out_specs=out_specs,
        out_shape=out_shape,
        scratch_shapes=[pltpu.VMEM((tm, d), BF16)],
        compiler_params=_params("arbitrary", "arbitrary"),
        name="inproj_full" if full else "inproj_rec",
    )(*args)


LEVEL_HALVES = (32, 16, 8, 4, 2, 1)


def _scan_consts():
    t = jnp.arange(CHUNK)
    tril = (t[:, None] >= t[None, :])
    cums = jnp.stack([tril, tril.T]).astype(BF16)
    masks = []
    for half in LEVEL_HALVES:
        blk = 2 * half
        same = (t[:, None] // blk) == (t[None, :] // blk)
        fwd = same & ((t[:, None] % blk) >= half) & ((t[None, :] % blk) < half)
        masks.append(jnp.stack([fwd, fwd.T]))
    masks = jnp.stack(masks, axis=1).astype(F32)
    return cums, masks


def _block_reference_rows(b, half, ridx):
    blk = 2 * half
    n = CHUNK // blk
    width = b.shape[1]
    if blk >= 8:
        parts = [jnp.broadcast_to(b[i * blk + ridx:i * blk + ridx + 1, :], (blk, width)) for i in range(n)]
        return parts[0] if n == 1 else jnp.concatenate(parts, axis=0)
    pos = lax.broadcasted_iota(jnp.int32, b.shape, 0) & (blk - 1)
    r = b
    for p in range(blk):
        delta = ridx - p
        if delta != 0:
            r = jnp.where(pos == p, pltpu.roll(b, (-delta) % CHUNK, axis=0), r)
    return r


def _neg_abs(x):
    bits = lax.bitcast_convert_type(x, jnp.uint32) | jnp.uint32(0x80000000)
    return lax.bitcast_convert_type(bits, F32)


def _query_key_rows(rev, half, q, k):
    blk = 2 * half
    if half >= 8:
        parts = []
        for i in range(CHUNK // blk):
            lo, mid, hi = i * blk, i * blk + half, (i + 1) * blk
            parts += [q[lo:mid], k[mid:hi]] if rev else [k[lo:mid], q[mid:hi]]
        return jnp.concatenate(parts, axis=0)
    pos = lax.broadcasted_iota(jnp.int32, q.shape, 0) & (blk - 1)
    return jnp.where((pos < half) if rev else (pos >= half), q, k)


def _scan_cumulative(d, g_ref, rows, cum, b_ref):
    g = g_ref[rows, :]
    g_hi = g.astype(BF16)
    g_lo = (g - g_hi.astype(F32)).astype(BF16)
    b_ref[d] = (jnp.dot(cum, g_hi, preferred_element_type=F32) + jnp.dot(cum, g_lo, preferred_element_type=F32)) * LOG2E


def _scan_scores(d, h, q_ref, g_ref, v_ref, rows, masks_ref, b_ref, pos):
    rev = d == 1
    sl = slice(h * HEAD_DIM, (h + 1) * HEAD_DIM)
    end = 0 if rev else CHUNK - 1
    q = q_ref[rows, sl]
    v = v_ref[rows, sl]
    f = jnp.exp(g_ref[rows, sl])
    k = 1.0 - f
    qf = q * f
    b = b_ref[d, :, sl]
    b_end = b_ref[d, end:end + 1, sl]
    scores = None
    for lvl, half in enumerate(LEVEL_HALVES):
        if half > 2:
            r = _block_reference_rows(b, half, half if rev else half - 1)
            xl = _query_key_rows(rev, half, q, k) * jnp.exp2(_neg_abs(b - r))
        elif half == 2:
            f_prev = pltpu.roll(f, 1, axis=0)
            f_next = pltpu.roll(f, CHUNK - 1, axis=0)
            if rev:
                xl = jnp.where(pos["q4"], qf * jnp.where(pos["first4"], f_next, 1.0), k * jnp.where(pos["last4"], f_prev, 1.0))
            else:
                xl = jnp.where(pos["q4"], qf * jnp.where(pos["last4"], f_prev, 1.0), k * jnp.where(pos["first4"], f_next, 1.0))
        else:
            xl = jnp.where(pos["q2"], qf, k)
        xl = xl.astype(BF16)
        a = lax.dot_general(xl, xl, (((1,), (1,)), ((), ())), preferred_element_type=F32) * masks_ref[d, lvl]
        scores = a if scores is None else scores + a
    qb = (q * jnp.exp2(b)).astype(BF16)
    kdec = (k * jnp.exp2(b_end - b)).astype(BF16)
    own = jnp.sum(q * k, axis=-1, keepdims=True) * v
    return scores.astype(BF16), qb, kdec, v.astype(BF16), own, jnp.exp2(b_end)


def _scan_state(d, h, staged, rows, s_ref, o_ref):
    p16, qb, kdec, v16, own, decay = staged
    sl = slice(h * HEAD_DIM, (h + 1) * HEAD_DIM)
    st = s_ref[d, h]
    o_inter = lax.dot_general(qb, st.astype(BF16), (((1,), (1,)), ((), ())), preferred_element_type=F32)
    o_intra = jnp.dot(p16, v16, preferred_element_type=F32)
    o_ref[rows, sl] = (o_inter + o_intra + own).astype(o_ref.dtype)
    upd = lax.dot_general(v16, kdec, (((0,), (0,)), ((), ())), preferred_element_type=F32)
    s_ref[d, h] = st * decay + upd


SCAN_LAG = 3


def _scan_kernel(qf_ref, gf_ref, vf_ref, qb_ref, gb_ref, vb_ref, s0_ref, cum_ref, masks_ref,
                 of_ref, ob_ref, sfin_ref, s_ref, b_ref, *, n_sub):
    c = pl.program_id(0)

    @pl.when(c == 0)
    def _():
        s_ref[...] = s0_ref[...]

    def body(cc, carry):
        rows = (pl.ds(pl.multiple_of(cc * CHUNK, CHUNK), CHUNK),
                pl.ds(pl.multiple_of((n_sub - 1 - cc) * CHUNK, CHUNK), CHUNK))
        ins = ((qf_ref, gf_ref, vf_ref), (qb_ref, gb_ref, vb_ref))
        outs = (of_ref, ob_ref)
        for d in range(2):
            _scan_cumulative(d, ins[d][1], rows[d], cum_ref[d], b_ref)
        r4 = lax.broadcasted_iota(jnp.int32, (CHUNK, HEAD_DIM), 0) & 3
        pos = [dict(q4=(r4 < 2) if d else (r4 >= 2), first4=r4 == 0, last4=r4 == 3,
                    q2=((r4 & 1) == 0) if d else ((r4 & 1) == 1)) for d in range(2)]
        chains = [(d, h) for h in range(N_HEADS) for d in range(2)]
        staged = {}
        for n in range(len(chains) + SCAN_LAG):
            if n < len(chains):
                d, h = chains[n]
                staged[n] = _scan_scores(d, h, *ins[d], rows[d], masks_ref, b_ref, pos[d])
            if n >= SCAN_LAG:
                d, h = chains[n - SCAN_LAG]
                _scan_state(d, h, staged.pop(n - SCAN_LAG), rows[d], s_ref, outs[d])
        return carry

    lax.fori_loop(0, n_sub, body, 0)

    @pl.when(c == pl.num_programs(0) - 1)
    def _():
        sfin_ref[...] = s_ref[...]


def _scan(rec, s0, n_sub):
    L = rec.shape[0]
    ns = L // (CHUNK * n_sub)
    cums, masks = _scan_consts()
    blk = (CHUNK * n_sub, SEG)
    state_spec = pl.BlockSpec(s0.shape, lambda c: (0, 0, 0, 0))
    return pl.pallas_call(
        functools.partial(_scan_kernel, n_sub=n_sub),
        grid=(ns,),
        in_specs=[pl.BlockSpec(blk, lambda c: (c, 0)),
                  pl.BlockSpec(blk, lambda c: (c, 1)),
                  pl.BlockSpec(blk, lambda c: (c, 3)),
                  pl.BlockSpec(blk, lambda c: (ns - 1 - c, 0)),
                  pl.BlockSpec(blk, lambda c: (ns - 1 - c, 2)),
                  pl.BlockSpec(blk, lambda c: (ns - 1 - c, 3)),
                  state_spec,
                  pl.BlockSpec(cums.shape, lambda c: (0, 0, 0)),
                  pl.BlockSpec(masks.shape, lambda c: (0, 0, 0, 0))],
        out_specs=[pl.BlockSpec(blk, lambda c: (c, 0)),
                   pl.BlockSpec(blk, lambda c: (ns - 1 - c, 0)),
                   state_spec],
        out_shape=[jax.ShapeDtypeStruct((L, A_WIDTH), BF16),
                   jax.ShapeDtypeStruct((L, A_WIDTH), BF16),
                   jax.ShapeDtypeStruct(s0.shape, F32)],
        scratch_shapes=[pltpu.VMEM(s0.shape, F32), pltpu.VMEM((2, CHUNK, A_WIDTH), F32)],
        compiler_params=_params("arbitrary"),
        name=f"scan_{ns}",
    )(rec, rec, rec, rec, rec, rec, s0, cums, masks)


def _mixout_kernel(of_ref, ob_ref, sg_ref, sb_ref, u_ref, up_ref, un_ref, ga_ref, gb_ref, x_ref,
                   g1_ref, sh2_ref, sc2_ref, n2_ref, ag_ref, cw_ref, wpa_ref, wpb_ref, wo_ref,
                   x1_ref, h2_ref, ya_ref):
    i = pl.program_id(0)
    tm = x_ref.shape[0]
    for h in range(N_HEADS):
        sl = slice(h * HEAD_DIM, (h + 1) * HEAD_DIM)
        o = of_ref[:, sl].astype(F32) + ob_ref[:, sl].astype(F32)
        o = o * lax.rsqrt(jnp.mean(o * o, axis=-1, keepdims=True) + EPS) * ag_ref[...]
        ya_ref[:, sl] = (o * sg_ref[:, sl].astype(F32)).astype(BF16)
    u = u_ref[...].astype(F32)
    rows = lax.broadcasted_iota(jnp.int32, u.shape, 0)
    prev_row = jnp.where(i > 0, up_ref[...].astype(F32)[BF16_ROWS - 1:BF16_ROWS, :], 0.0)
    next_row = jnp.where(i < pl.num_programs(0) - 1, un_ref[...].astype(F32)[0:1, :], 0.0)
    u_prev = jnp.where(rows == 0, prev_row, pltpu.roll(u, 1, axis=0))
    u_next = jnp.where(rows == tm - 1, next_row, pltpu.roll(u, tm - 1, axis=0))
    yb = sb_ref[...].astype(F32) * (cw_ref[0:1, :] * u_prev + cw_ref[1:2, :] * u + cw_ref[2:3, :] * u_next)
    pa = jnp.dot(ya_ref[...], wpa_ref[...], preferred_element_type=F32)
    pb = jnp.dot(yb.astype(BF16), wpb_ref[...], preferred_element_type=F32)
    merged = (ga_ref[...].astype(F32) * pa + gb_ref[...].astype(F32) * pb).astype(BF16)
    out = jnp.dot(merged, wo_ref[...], preferred_element_type=F32)
    x1 = x_ref[...] + g1_ref[0:1, :] * out
    x1_ref[...] = x1
    y = x1 * lax.rsqrt(jnp.mean(x1 * x1, axis=-1, keepdims=True) + EPS) * n2_ref[...]
    h2_ref[...] = (y * (1.0 + sc2_ref[0:1, :]) + sh2_ref[0:1, :]).astype(BF16)


def _mixout(o_f, o_b, rest, x2d, mod, n2g, ag, cw, wpa, wpb, wo, *, tm):
    L, d = x2d.shape
    nt = L // tm
    rt = tm // BF16_ROWS
    last_t = L // BF16_ROWS - 1
    row = lambda i: (i, 0)
    return pl.pallas_call(
        _mixout_kernel,
        grid=(nt,),
        in_specs=[pl.BlockSpec((tm, A_WIDTH), row),
                  pl.BlockSpec((tm, A_WIDTH), row),
                  pl.BlockSpec((tm, SEG), lambda i: (i, REST_G)),
                  pl.BlockSpec((tm, SEG), lambda i: (i, REST_B)),
                  pl.BlockSpec((tm, SEG), lambda i: (i, REST_U)),
                  pl.BlockSpec((BF16_ROWS, SEG), lambda i: (jnp.maximum(i * rt - 1, 0), REST_U)),
                  pl.BlockSpec((BF16_ROWS, SEG), lambda i: (jnp.minimum((i + 1) * rt, last_t), REST_U)),
                  pl.BlockSpec((tm, d), lambda i: (i, REST_GATE_A // 2)),
                  pl.BlockSpec((tm, d), lambda i: (i, REST_GATE_B // 2)),
                  pl.BlockSpec((tm, d), row),
                  pl.BlockSpec((8, d), lambda i: (0, 2)),
                  pl.BlockSpec((8, d), lambda i: (0, 3)),
                  pl.BlockSpec((8, d), lambda i: (0, 4)),
                  pl.BlockSpec((1, d), lambda i: (0, 0)),
                  pl.BlockSpec((1, HEAD_DIM), lambda i: (0, 0)),
                  pl.BlockSpec(cw.shape, lambda i: (0, 0)),
                  _const_spec(wpa.shape), _const_spec(wpb.shape), _const_spec(wo.shape)],
        out_specs=[pl.BlockSpec((tm, d), row), pl.BlockSpec((tm, d), row)],
        out_shape=[jax.ShapeDtypeStruct((L, d), F32), jax.ShapeDtypeStruct((L, d), BF16)],
        scratch_shapes=[pltpu.VMEM((tm, A_WIDTH), BF16)],
        compiler_params=_params("arbitrary"),
        name="mixer_out",
    )(o_f, o_b, rest, rest, rest, rest, rest, rest, rest, x2d, mod, mod, mod, n2g, ag, cw, wpa, wpb, wo)


def _ffn_kernel(h_ref, hb_ref, x1_ref, wa_ref, wb_ref, wd_ref, dw_ref, db_ref, g2_ref, fg_ref,
                o_ref, hx_ref, edge_ref, *, sub):
    i = pl.program_id(0)
    f = pl.program_id(1)
    tm = h_ref.shape[0]
    tf = wa_ref.shape[1]
    ext = tm + 2 * GRID_W

    @pl.when(f == 0)
    def _():
        hx_ref[0:tm, :] = h_ref[...]
        hx_ref[tm:tm + GRID_W, :] = jnp.where(i < pl.num_programs(0) - 1, hb_ref[...], jnp.zeros(hb_ref.shape, BF16))
        o_ref[...] = jnp.zeros(o_ref.shape, F32)

    @pl.when((f == 0) & (i == 0))
    def _():
        edge_ref[...] = jnp.zeros(edge_ref.shape, F32)

    ng = tm // GRID_W
    n_sub = tf // sub
    a_all = [jnp.dot(hx_ref[...], wa_ref[:, s * sub:(s + 1) * sub], preferred_element_type=F32) for s in range(n_sub)]
    b_all = [jnp.dot(hx_ref[0:tm, :], wb_ref[:, s * sub:(s + 1) * sub], preferred_element_type=F32)
             for s in range(n_sub)]
    colg = lax.broadcasted_iota(jnp.int32, (GRID_W, sub), 0)
    acc = None
    for s in range(n_sub):
        cs = slice(s * sub, (s + 1) * sub)
        a = jnp.concatenate([edge_ref[f, :, cs], a_all[s]], axis=0)
        edge_ref[f, :, cs] = a_all[s][tm - GRID_W:tm]
        a3 = a.reshape(ng + 2, GRID_W, sub)
        l3 = pltpu.roll(a, 1, axis=0).reshape(ng + 2, GRID_W, sub)
        r3 = pltpu.roll(a, ext - 1, axis=0).reshape(ng + 2, GRID_W, sub)
        conv = None
        for dr in range(3):
            w_l = jnp.where(colg == 0, 0.0, dw_ref[3 * dr:3 * dr + 1, cs])
            w_c = dw_ref[3 * dr + 1:3 * dr + 2, cs]
            w_r = jnp.where(colg == GRID_W - 1, 0.0, dw_ref[3 * dr + 2:3 * dr + 3, cs])
            term = w_l * l3[dr:dr + ng] + w_c * a3[dr:dr + ng] + w_r * r3[dr:dr + ng]
            conv = term if conv is None else conv + term
        hid = (_silu(conv + db_ref[:, cs]) * b_all[s].reshape(ng, GRID_W, sub)).astype(BF16).reshape(tm, sub)
        part = jnp.dot(hid, wd_ref[cs, :], preferred_element_type=F32)
        acc = part if acc is None else acc + part
    o_ref[...] += acc

    @pl.when(f == pl.num_programs(1) - 1)
    def _():
        x2 = x1_ref[...] + g2_ref[0:1, :] * o_ref[...]
        o_ref[...] = x2 * lax.rsqrt(jnp.mean(x2 * x2, axis=-1, keepdims=True) + EPS) * fg_ref[...]


def _ffn(h2, x1, mod, w_up, w_down, dw9, db, fg, *, tm, tf, sub):
    L, d = x1.shape
    nt = L // tm
    nf = D_FF // tf
    rg = tm // GRID_W
    lastg = L // GRID_W - 1
    return pl.pallas_call(
        functools.partial(_ffn_kernel, sub=sub),
        grid=(nt, nf),
        in_specs=[pl.BlockSpec((tm, d), lambda i, f: (i, 0)),
                  pl.BlockSpec((GRID_W, d), lambda i, f: (jnp.minimum((i + 1) * rg, lastg), 0)),
                  pl.BlockSpec((tm, d), lambda i, f: (i, 0)),
                  pl.BlockSpec((d, tf), lambda i, f: (0, f)),
                  pl.BlockSpec((d, tf), lambda i, f: (0, nf + f)),
                  pl.BlockSpec((tf, d), lambda i, f: (f, 0)),
                  pl.BlockSpec((9, tf), lambda i, f: (0, f)),
                  pl.BlockSpec((1, tf), lambda i, f: (0, f)),
                  pl.BlockSpec((8, d), lambda i, f: (0, 5)),
                  pl.BlockSpec((1, d), lambda i, f: (0, 0))],
        out_specs=pl.BlockSpec((tm, d), lambda i, f: (i, 0)),
        out_shape=jax.ShapeDtypeStruct((L, d), F32),
        scratch_shapes=[pltpu.VMEM((tm + GRID_W, d), BF16),
                        pltpu.VMEM((nf, GRID_W, tf), F32)],
        compiler_params=_params("arbitrary", "arbitrary"),
        name="conv_ffn",
    )(h2, h2, x1, w_up, w_up, w_down, dw9, db, mod, fg)


def kernel(x, c, ctx, c_ctx, w_mod, b_mod, norm1_g, w_in, lb_raw, a_norm_g, sconv_w, w_pa, w_pb, w_o,
           norm2_g, w_up, ffn_dw, ffn_db, w_down, final_g):
    assert x.shape[0] == 1 and w_in.shape[0] == 1, "single-sequence, depth-1 layer"
    d = D_MODEL
    x2d = x[0]
    ctx2d = ctx[0]
    t_lat = _tiles(x2d.shape[0])
    t_ctx = _tiles(ctx2d.shape[0])

    cc = jnp.zeros((16, d), F32).at[0].set(c[0]).at[1].set(c_ctx)
    mod = _modulation(cc, w_mod[0], b_mod[0][None, :], t_lat["mod_tn"])
    lb = _lower_bounds(lb_raw.reshape(lb_raw.shape[0], 2 * A_WIDTH))

    w_in16 = w_in[0].astype(BF16)
    g1n = norm1_g[0][None, :]
    (rec_c,) = _inproj(ctx2d, mod, g1n, w_in16, lb, mod_row=1, full=False, tm=t_ctx["ctx_tm"])
    s_zero = jnp.zeros((2, N_HEADS, HEAD_DIM, HEAD_DIM), F32)
    _, _, s_ctx = _scan(rec_c, s_zero, t_ctx["scan_chunks"])
    rec, rest = _inproj(x2d, mod, g1n, w_in16, lb, mod_row=0, full=True, tm=t_lat["inproj_tm"])
    o_f, o_b, _ = _scan(rec, s_ctx, t_lat["scan_chunks"])
    x1, h2 = _mixout(o_f, o_b, rest, x2d, mod, norm2_g[0][None, :], a_norm_g[0][None, :], sconv_w[0],
                     w_pa[0].astype(BF16), w_pb[0].astype(BF16), w_o[0].astype(BF16), tm=t_lat["mix_tm"])
    out = _ffn(h2, x1, mod, w_up[0].astype(BF16), w_down[0].astype(BF16), ffn_dw[0].reshape(9, D_FF),
               ffn_db[0][None, :], final_g[None, :], tm=t_lat["ffn_tm"], tf=t_lat["ffn_tf"], sub=t_lat["ffn_sub"])
    return out[None]
```

```python
import functools
import math

import jax
import jax.numpy as jnp
from jax import lax
from jax.experimental import pallas as pl
from jax.experimental.pallas import tpu as pltpu

D_MODEL = 2048
N_HEADS = 8
HEAD_DIM = 128
A_WIDTH = N_HEADS * HEAD_DIM
B_WIDTH = 1024
D_FF = 5632
CHUNK = 64
GRID_W = 64
EPS = 1e-6
SEG = 1024
N_SEG_REC = 4
SEG_H = 7
REST_GATE_A, REST_GATE_B, REST_G, REST_B, REST_U, N_REST = 0, 2, 4, 5, 6, 7
V7X_VMEM_BYTES = 64 * 1024 * 1024
VMEM_LIMIT = V7X_VMEM_BYTES - 8 * 1024 * 1024
BF16_ROWS = 16
LOG2E = math.log2(math.e)
BF16 = jnp.bfloat16
F32 = jnp.float32


def _tiles(L):
    return dict(inproj_tm=min(L, 1024), mix_tm=min(L, 256), ctx_tm=min(L, 256), ffn_tm=min(L, 512), ffn_tf=512, ffn_sub=256,
                scan_chunks=min(L // CHUNK, 8), mod_tn=512)


def _sigmoid(z):
    return 0.5 * jnp.tanh(0.5 * z) + 0.5


def _silu(z):
    return z * _sigmoid(z)


def _const_spec(shape, index=None):
    idx = index if index is not None else (0,) * len(shape)
    return pl.BlockSpec(shape, lambda *_: idx, pipeline_mode=pl.Buffered(1))


def _params(*semantics):
    return pltpu.CompilerParams(dimension_semantics=semantics, vmem_limit_bytes=VMEM_LIMIT)


def _mod_kernel(cc_ref, w_ref, b_ref, o_ref):
    s = _silu(cc_ref[...]).astype(BF16)
    o_ref[...] = jnp.dot(s, w_ref[...].astype(BF16), preferred_element_type=F32) + b_ref[...]


def _modulation(cc, w_mod, b_mod, tn):
    rows, d = cc.shape
    n = w_mod.shape[1]
    return pl.pallas_call(
        _mod_kernel,
        grid=(n // tn,),
        in_specs=[pl.BlockSpec((rows, d), lambda j: (0, 0)),
                  pl.BlockSpec((d, tn), lambda j: (0, j)),
                  pl.BlockSpec((1, tn), lambda j: (0, j))],
        out_specs=pl.BlockSpec((rows, tn), lambda j: (0, j)),
        out_shape=jax.ShapeDtypeStruct((rows, n), F32),
        compiler_params=_params("arbitrary"),
        name="modulation",
    )(cc, w_mod, b_mod)


def _lb_kernel(raw_ref, o_ref):
    raw = raw_ref[...]
    e = jnp.exp(raw - jnp.max(raw, axis=0, keepdims=True))
    o_ref[...] = e[0:1, :] / jnp.sum(e, axis=0, keepdims=True)


def _lower_bounds(lb_raw2d):
    return pl.pallas_call(
        _lb_kernel,
        out_shape=jax.ShapeDtypeStruct((1, lb_raw2d.shape[1]), F32),
        name="lower_bounds",
    )(lb_raw2d)


def _inproj_kernel(*refs, mod_row, full):
    if full:
        x_ref, sh_ref, sc_ref, g_ref, w_ref, wh_ref, lb_ref, rec_ref, rest_ref, hx_ref = refs
    else:
        x_ref, sh_ref, sc_ref, g_ref, w_ref, lb_ref, rec_ref, hx_ref = refs
    j = pl.program_id(1)

    @pl.when(j == 0)
    def _():
        x = x_ref[...]
        y = x * lax.rsqrt(jnp.mean(x * x, axis=-1, keepdims=True) + EPS) * g_ref[...]
        sh = sh_ref[mod_row:mod_row + 1, :]
        sc = sc_ref[mod_row:mod_row + 1, :]
        hx_ref[...] = (y * (1.0 + sc) + sh).astype(BF16)

    tm = hx_ref.shape[0]
    halves = [slice(r * (tm // 2), (r + 1) * (tm // 2)) for r in range(2)]

    def z(rows, wr=w_ref):
        return jnp.dot(hx_ref[rows, :], wr[...], preferred_element_type=F32)

    @pl.when(j == 0)
    def _():
        for rows in halves:
            rec_ref[rows, :] = _silu(z(rows))

    @pl.when((j == 1) | (j == 2))
    def _():
        lb = jnp.where(j == 1, lb_ref[:, 0:SEG], lb_ref[:, SEG:2 * SEG])
        for rows in halves:
            rec_ref[rows, :] = jnp.log(lb + (1.0 - lb) * _sigmoid(z(rows)))

    @pl.when(j == 3)
    def _():
        for rows in halves:
            rec_ref[rows, :] = z(rows)

    if full:
        @pl.when(j == 4)
        def _():
            for rows in halves:
                rest_ref[rows, :] = _silu(z(rows)).astype(BF16)

        @pl.when(j == 5)
        def _():
            for rows in halves:
                rest_ref[rows, :] = z(rows).astype(BF16)

        @pl.when(j == 6)
        def _():
            for rows in halves:
                rest_ref[rows, :] = (z(rows) * z(rows, wh_ref)).astype(BF16)

        @pl.when(j >= 7)
        def _():
            for rows in halves:
                rest_ref[rows, :] = _sigmoid(z(rows)).astype(BF16)


def _inproj(x2d, mod, g, w, lb, *, mod_row, full, tm):
    L, d = x2d.shape
    n_steps = 11 if full else N_SEG_REC
    w_seg = lambda i, j: (0, jnp.where(j >= SEG_H, j + 1, j))
    rest_blk = lambda i, j: (i, jnp.where(j < REST_G, REST_G, jnp.where(j <= REST_U, j, j - 7)))
    in_specs = [pl.BlockSpec((tm, d), lambda i, j: (i, 0)),
                pl.BlockSpec((8, d), lambda i, j: (0, 0)),
                pl.BlockSpec((8, d), lambda i, j: (0, 1)),
                pl.BlockSpec((1, d), lambda i, j: (0, 0)),
                pl.BlockSpec((d, SEG), w_seg)]
    args = [x2d, mod, mod, g, w]
    out_specs = [pl.BlockSpec((tm, SEG), lambda i, j: (i, jnp.minimum(j, N_SEG_REC - 1)))]
    out_shape = [jax.ShapeDtypeStruct((L, N_SEG_REC * SEG), F32)]
    if full:
        in_specs.append(_const_spec((d, SEG), (0, SEG_H)))
        args.append(w)
        out_specs.append(pl.BlockSpec((tm, SEG), rest_blk))
        out_shape.append(jax.ShapeDtypeStruct((L, N_REST * SEG), BF16))
    in_specs.append(pl.BlockSpec((1, 2 * SEG), lambda i, j: (0, 0)))
    args.append(lb)
    return pl.pallas_call(
        functools.partial(_inproj_kernel, mod_row=mod_row, full=full),
        grid=(L // tm, n_steps),
        in_specs=in_specs,
        out_specs=out_specs,
        out_shape=out_shape,
        scratch_shapes=[pltpu.VMEM((tm, d), BF16)],
        compiler_params=_params("arbitrary", "arbitrary"),
        name="inproj_full" if full else "inproj_rec",
    )(*args)


LEVEL_HALVES = (32, 16, 8, 4, 2, 1)


def _scan_consts():
    t = jnp.arange(CHUNK)
    tril = (t[:, None] >= t[None, :])
    cums = jnp.stack([tril, tril.T]).astype(BF16)
    masks = []
    for half in LEVEL_HALVES:
        blk = 2 * half
        same = (t[:, None] // blk) == (t[None, :] // blk)
        fwd = same & ((t[:, None] % blk) >= half) & ((t[None, :] % blk) < half)
        masks.append(jnp.stack([fwd, fwd.T]))
    masks = jnp.stack(masks, axis=1).astype(F32)
    return cums, masks


def _block_reference_rows(b, half, ridx):
    blk = 2 * half
    n = CHUNK // blk
    width = b.shape[1]
    if blk >= 8:
        parts = [jnp.broadcast_to(b[i * blk + ridx:i * blk + ridx + 1, :], (blk, width)) for i in range(n)]
        return parts[0] if n == 1 else jnp.concatenate(parts, axis=0)
    pos = lax.broadcasted_iota(jnp.int32, b.shape, 0) & (blk - 1)
    r = b
    for p in range(blk):
        delta = ridx - p
        if delta != 0:
            r = jnp.where(pos == p, pltpu.roll(b, (-delta) % CHUNK, axis=0), r)
    return r


def _neg_abs(x):
    bits = lax.bitcast_convert_type(x, jnp.uint32) | jnp.uint32(0x80000000)
    return lax.bitcast_convert_type(bits, F32)


def _query_key_rows(rev, half, q, k):
    blk = 2 * half
    if half >= 8:
        parts = []
        for i in range(CHUNK // blk):
            lo, mid, hi = i * blk, i * blk + half, (i + 1) * blk
            parts += [q[lo:mid], k[mid:hi]] if rev else [k[lo:mid], q[mid:hi]]
        return jnp.concatenate(parts, axis=0)
    pos = lax.broadcasted_iota(jnp.int32, q.shape, 0) & (blk - 1)
    return jnp.where((pos < half) if rev else (pos >= half), q, k)


def _scan_cumulative(d, g_ref, rows, cum, b_ref):
    g = g_ref[rows, :]
    g_hi = g.astype(BF16)
    g_lo = (g - g_hi.astype(F32)).astype(BF16)
    b_ref[d] = (jnp.dot(cum, g_hi, preferred_element_type=F32) + jnp.dot(cum, g_lo, preferred_element_type=F32)) * LOG2E


def _scan_scores(d, h, q_ref, g_ref, v_ref, rows, masks_ref, b_ref, pos):
    rev = d == 1
    sl = slice(h * HEAD_DIM, (h + 1) * HEAD_DIM)
    end = 0 if rev else CHUNK - 1
    q = q_ref[rows, sl]
    v = v_ref[rows, sl]
    f = jnp.exp(g_ref[rows, sl])
    k = 1.0 - f
    qf = q * f
    b = b_ref[d, :, sl]
    b_end = b_ref[d, end:end + 1, sl]
    scores = None
    for lvl, half in enumerate(LEVEL_HALVES):
        if half > 2:
            r = _block_reference_rows(b, half, half if rev else half - 1)
            xl = _query_key_rows(rev, half, q, k) * jnp.exp2(_neg_abs(b - r))
        elif half == 2:
            f_prev = pltpu.roll(f, 1, axis=0)
            f_next = pltpu.roll(f, CHUNK - 1, axis=0)
            if rev:
                xl = jnp.where(pos["q4"], qf * jnp.where(pos["first4"], f_next, 1.0), k * jnp.where(pos["last4"], f_prev, 1.0))
            else:
                xl = jnp.where(pos["q4"], qf * jnp.where(pos["last4"], f_prev, 1.0), k * jnp.where(pos["first4"], f_next, 1.0))
        else:
            xl = jnp.where(pos["q2"], qf, k)
        xl = xl.astype(BF16)
        a = lax.dot_general(xl, xl, (((1,), (1,)), ((), ())), preferred_element_type=F32) * masks_ref[d, lvl]
        scores = a if scores is None else scores + a
    qb = (q * jnp.exp2(b)).astype(BF16)
    kdec = (k * jnp.exp2(b_end - b)).astype(BF16)
    own = jnp.sum(q * k, axis=-1, keepdims=True) * v
    return scores.astype(BF16), qb, kdec, v.astype(BF16), own, jnp.exp2(b_end)


def _scan_state(d, h, staged, rows, s_ref, o_ref):
    p16, qb, kdec, v16, own, decay = staged
    sl = slice(h * HEAD_DIM, (h + 1) * HEAD_DIM)
    st = s_ref[d, h]
    o_inter = lax.dot_general(qb, st.astype(BF16), (((1,), (1,)), ((), ())), preferred_element_type=F32)
    o_intra = jnp.dot(p16, v16, preferred_element_type=F32)
    o_ref[rows, sl] = (o_inter + o_intra + own).astype(o_ref.dtype)
    upd = lax.dot_general(v16, kdec, (((0,), (0,)), ((), ())), preferred_element_type=F32)
    s_ref[d, h] = st * decay + upd


SCAN_LAG = 3


def _scan_kernel(qf_ref, gf_ref, vf_ref, qb_ref, gb_ref, vb_ref, s0_ref, cum_ref, masks_ref,
                 of_ref, ob_ref, sfin_ref, s_ref, b_ref, *, n_sub):
    c = pl.program_id(0)

    @pl.when(c == 0)
    def _():
        s_ref[...] = s0_ref[...]

    def body(cc, carry):
        rows = (pl.ds(pl.multiple_of(cc * CHUNK, CHUNK), CHUNK),
                pl.ds(pl.multiple_of((n_sub - 1 - cc) * CHUNK, CHUNK), CHUNK))
        ins = ((qf_ref, gf_ref, vf_ref), (qb_ref, gb_ref, vb_ref))
        outs = (of_ref, ob_ref)
        for d in range(2):
            _scan_cumulative(d, ins[d][1], rows[d], cum_ref[d], b_ref)
        r4 = lax.broadcasted_iota(jnp.int32, (CHUNK, HEAD_DIM), 0) & 3
        pos = [dict(q4=(r4 < 2) if d else (r4 >= 2), first4=r4 == 0, last4=r4 == 3,
                    q2=((r4 & 1) == 0) if d else ((r4 & 1) == 1)) for d in range(2)]
        chains = [(d, h) for h in range(N_HEADS) for d in range(2)]
        staged = {}
        for n in range(len(chains) + SCAN_LAG):
            if n < len(chains):
                d, h = chains[n]
                staged[n] = _scan_scores(d, h, *ins[d], rows[d], masks_ref, b_ref, pos[d])
            if n >= SCAN_LAG:
                d, h = chains[n - SCAN_LAG]
                _scan_state(d, h, staged.pop(n - SCAN_LAG), rows[d], s_ref, outs[d])
        return carry

    lax.fori_loop(0, n_sub, body, 0)

    @pl.when(c == pl.num_programs(0) - 1)
    def _():
        sfin_ref[...] = s_ref[...]


def _scan(rec, s0, n_sub):
    L = rec.shape[0]
    ns = L // (CHUNK * n_sub)
    cums, masks = _scan_consts()
    blk = (CHUNK * n_sub, SEG)
    state_spec = pl.BlockSpec(s0.shape, lambda c: (0, 0, 0, 0))
    return pl.pallas_call(
        functools.partial(_scan_kernel, n_sub=n_sub),
        grid=(ns,),
        in_specs=[pl.BlockSpec(blk, lambda c: (c, 0)),
                  pl.BlockSpec(blk, lambda c: (c, 1)),
                  pl.BlockSpec(blk, lambda c: (c, 3)),
                  pl.BlockSpec(blk, lambda c: (ns - 1 - c, 0)),
                  pl.BlockSpec(blk, lambda c: (ns - 1 - c, 2)),
                  pl.BlockSpec(blk, lambda c: (ns - 1 - c, 3)),
                  state_spec,
                  pl.BlockSpec(cums.shape, lambda c: (0, 0, 0)),
                  pl.BlockSpec(masks.shape, lambda c: (0, 0, 0, 0))],
        out_specs=[pl.BlockSpec(blk, lambda c: (c, 0)),
                   pl.BlockSpec(blk, lambda c: (ns - 1 - c, 0)),
                   state_spec],
        out_shape=[jax.ShapeDtypeStruct((L, A_WIDTH), BF16),
                   jax.ShapeDtypeStruct((L, A_WIDTH), BF16),
                   jax.ShapeDtypeStruct(s0.shape, F32)],
        scratch_shapes=[pltpu.VMEM(s0.shape, F32), pltpu.VMEM((2, CHUNK, A_WIDTH), F32)],
        compiler_params=_params("arbitrary"),
        name=f"scan_{ns}",
    )(rec, rec, rec, rec, rec, rec, s0, cums, masks)


def _mixout_kernel(of_ref, ob_ref, sg_ref, sb_ref, u_ref, up_ref, un_ref, ga_ref, gb_ref, x_ref,
                   g1_ref, sh2_ref, sc2_ref, n2_ref, ag_ref, cw_ref, wpa_ref, wpb_ref, wo_ref,
                   x1_ref, h2_ref, ya_ref):
    i = pl.program_id(0)
    tm = x_ref.shape[0]
    for h in range(N_HEADS):
        sl = slice(h * HEAD_DIM, (h + 1) * HEAD_DIM)
        o = of_ref[:, sl].astype(F32) + ob_ref[:, sl].astype(F32)
        o = o * lax.rsqrt(jnp.mean(o * o, axis=-1, keepdims=True) + EPS) * ag_ref[...]
        ya_ref[:, sl] = (o * sg_ref[:, sl].astype(F32)).astype(BF16)
    u = u_ref[...].astype(F32)
    rows = lax.broadcasted_iota(jnp.int32, u.shape, 0)
    prev_row = jnp.where(i > 0, up_ref[...].astype(F32)[BF16_ROWS - 1:BF16_ROWS, :], 0.0)
    next_row = jnp.where(i < pl.num_programs(0) - 1, un_ref[...].astype(F32)[0:1, :], 0.0)
    u_prev = jnp.where(rows == 0, prev_row, pltpu.roll(u, 1, axis=0))
    u_next = jnp.where(rows == tm - 1, next_row, pltpu.roll(u, tm - 1, axis=0))
    yb = sb_ref[...].astype(F32) * (cw_ref[0:1, :] * u_prev + cw_ref[1:2, :] * u + cw_ref[2:3, :] * u_next)
    pa = jnp.dot(ya_ref[...], wpa_ref[...], preferred_element_type=F32)
    pb = jnp.dot(yb.astype(BF16), wpb_ref[...], preferred_element_type=F32)
    merged = (ga_ref[...].astype(F32) * pa + gb_ref[...].astype(F32) * pb).astype(BF16)
    out = jnp.dot(merged, wo_ref[...], preferred_element_type=F32)
    x1 = x_ref[...] + g1_ref[0:1, :] * out
    x1_ref[...] = x1
    y = x1 * lax.rsqrt(jnp.mean(x1 * x1, axis=-1, keepdims=True) + EPS) * n2_ref[...]
    h2_ref[...] = (y * (1.0 + sc2_ref[0:1, :]) + sh2_ref[0:1, :]).astype(BF16)


def _mixout(o_f, o_b, rest, x2d, mod, n2g, ag, cw, wpa, wpb, wo, *, tm):
    L, d = x2d.shape
    nt = L // tm
    rt = tm // BF16_ROWS
    last_t = L // BF16_ROWS - 1
    row = lambda i: (i, 0)
    return pl.pallas_call(
        _mixout_kernel,
        grid=(nt,),
        in_specs=[pl.BlockSpec((tm, A_WIDTH), row),
                  pl.BlockSpec((tm, A_WIDTH), row),
                  pl.BlockSpec((tm, SEG), lambda i: (i, REST_G)),
                  pl.BlockSpec((tm, SEG), lambda i: (i, REST_B)),
                  pl.BlockSpec((tm, SEG), lambda i: (i, REST_U)),
                  pl.BlockSpec((BF16_ROWS, SEG), lambda i: (jnp.maximum(i * rt - 1, 0), REST_U)),
                  pl.BlockSpec((BF16_ROWS, SEG), lambda i: (jnp.minimum((i + 1) * rt, last_t), REST_U)),
                  pl.BlockSpec((tm, d), lambda i: (i, REST_GATE_A // 2)),
                  pl.BlockSpec((tm, d), lambda i: (i, REST_GATE_B // 2)),
                  pl.BlockSpec((tm, d), row),
                  pl.BlockSpec((8, d), lambda i: (0, 2)),
                  pl.BlockSpec((8, d), lambda i: (0, 3)),
                  pl.BlockSpec((8, d), lambda i: (0, 4)),
                  pl.BlockSpec((1, d), lambda i: (0, 0)),
                  pl.BlockSpec((1, HEAD_DIM), lambda i: (0, 0)),
                  pl.BlockSpec(cw.shape, lambda i: (0, 0)),
                  _const_spec(wpa.shape), _const_spec(wpb.shape), _const_spec(wo.shape)],
        out_specs=[pl.BlockSpec((tm, d), row), pl.BlockSpec((tm, d), row)],
        out_shape=[jax.ShapeDtypeStruct((L, d), F32), jax.ShapeDtypeStruct((L, d), BF16)],
        scratch_shapes=[pltpu.VMEM((tm, A_WIDTH), BF16)],
        compiler_params=_params("arbitrary"),
        name="mixer_out",
    )(o_f, o_b, rest, rest, rest, rest, rest, rest, rest, x2d, mod, mod, mod, n2g, ag, cw, wpa, wpb, wo)


def _ffn_kernel(h_ref, hb_ref, x1_ref, wa_ref, wb_ref, wd_ref, dw_ref, db_ref, g2_ref, fg_ref,
                o_ref, hx_ref, edge_ref, *, sub):
    i = pl.program_id(0)
    f = pl.program_id(1)
    tm = h_ref.shape[0]
    tf = wa_ref.shape[1]
    ext = tm + 2 * GRID_W

    @pl.when(f == 0)
    def _():
        hx_ref[0:tm, :] = h_ref[...]
        hx_ref[tm:tm + GRID_W, :] = jnp.where(i < pl.num_programs(0) - 1, hb_ref[...], jnp.zeros(hb_ref.shape, BF16))
        o_ref[...] = jnp.zeros(o_ref.shape, F32)

    @pl.when((f == 0) & (i == 0))
    def _():
        edge_ref[...] = jnp.zeros(edge_ref.shape, F32)

    ng = tm // GRID_W
    n_sub = tf // sub
    a_all = [jnp.dot(hx_ref[...], wa_ref[:, s * sub:(s + 1) * sub], preferred_element_type=F32) for s in range(n_sub)]
    b_all = [jnp.dot(hx_ref[0:tm, :], wb_ref[:, s * sub:(s + 1) * sub], preferred_element_type=F32)
             for s in range(n_sub)]
    colg = lax.broadcasted_iota(jnp.int32, (GRID_W, sub), 0)
    acc = None
    for s in range(n_sub):
        cs = slice(s * sub, (s + 1) * sub)
        a = jnp.concatenate([edge_ref[f, :, cs], a_all[s]], axis=0)
        edge_ref[f, :, cs] = a_all[s][tm - GRID_W:tm]
        a3 = a.reshape(ng + 2, GRID_W, sub)
        l3 = pltpu.roll(a, 1, axis=0).reshape(ng + 2, GRID_W, sub)
        r3 = pltpu.roll(a, ext - 1, axis=0).reshape(ng + 2, GRID_W, sub)
        conv = None
        for dr in range(3):
            w_l = jnp.where(colg == 0, 0.0, dw_ref[3 * dr:3 * dr + 1, cs])
            w_c = dw_ref[3 * dr + 1:3 * dr + 2, cs]
            w_r = jnp.where(colg == GRID_W - 1, 0.0, dw_ref[3 * dr + 2:3 * dr + 3, cs])
            term = w_l * l3[dr:dr + ng] + w_c * a3[dr:dr + ng] + w_r * r3[dr:dr + ng]
            conv = term if conv is None else conv + term
        hid = (_silu(conv + db_ref[:, cs]) * b_all[s].reshape(ng, GRID_W, sub)).astype(BF16).reshape(tm, sub)
        part = jnp.dot(hid, wd_ref[cs, :], preferred_element_type=F32)
        acc = part if acc is None else acc + part
    o_ref[...] += acc

    @pl.when(f == pl.num_programs(1) - 1)
    def _():
        x2 = x1_ref[...] + g2_ref[0:1, :] * o_ref[...]
        o_ref[...] = x2 * lax.rsqrt(jnp.mean(x2 * x2, axis=-1, keepdims=True) + EPS) * fg_ref[...]


def _ffn(h2, x1, mod, w_up, w_down, dw9, db, fg, *, tm, tf, sub):
    L, d = x1.shape
    nt = L // tm
    nf = D_FF // tf
    rg = tm // GRID_W
    lastg = L // GRID_W - 1
    return pl.pallas_call(
        functools.partial(_ffn_kernel, sub=sub),
        grid=(nt, nf),
        in_specs=[pl.BlockSpec((tm, d), lambda i, f: (i, 0)),
                  pl.BlockSpec((GRID_W, d), lambda i, f: (jnp.minimum((i + 1) * rg, lastg), 0)),
                  pl.BlockSpec((tm, d), lambda i, f: (i, 0)),
                  pl.BlockSpec((d, tf), lambda i, f: (0, f)),
                  pl.BlockSpec((d, tf), lambda i, f: (0, nf + f)),
                  pl.BlockSpec((tf, d), lambda i, f: (f, 0)),
                  pl.BlockSpec((9, tf), lambda i, f: (0, f)),
                  pl.BlockSpec((1, tf), lambda i, f: (0, f)),
                  pl.BlockSpec((8, d), lambda i, f: (0, 5)),
                  pl.BlockSpec((1, d), lambda i, f: (0, 0))],
        out_specs=pl.BlockSpec((tm, d), lambda i, f: (i, 0)),
        out_shape=jax.ShapeDtypeStruct((L, d), F32),
        scratch_shapes=[pltpu.VMEM((tm + GRID_W, d), BF16),
                        pltpu.VMEM((nf, GRID_W, tf), F32)],
        compiler_params=_params("arbitrary", "arbitrary"),
        name="conv_ffn",
    )(h2, h2, x1, w_up, w_up, w_down, dw9, db, mod, fg)


def kernel(x, c, ctx, c_ctx, w_mod, b_mod, norm1_g, w_in, lb_raw, a_norm_g, sconv_w, w_pa, w_pb, w_o,
           norm2_g, w_up, ffn_dw, ffn_db, w_down, final_g):
    assert x.shape[0] == 1 and w_in.shape[0] == 1, "single-sequence, depth-1 layer"
    d = D_MODEL
    x2d = x[0]
    ctx2d = ctx[0]
    t_lat = _tiles(x2d.shape[0])
    t_ctx = _tiles(ctx2d.shape[0])

    cc = jnp.zeros((16, d), F32).at[0].set(c[0]).at[1].set(c_ctx)
    mod = _modulation(cc, w_mod[0], b_mod[0][None, :], t_lat["mod_tn"])
    lb = _lower_bounds(lb_raw.reshape(lb_raw.shape[0], 2 * A_WIDTH))

    w_in16 = w_in[0].astype(BF16)
    g1n = norm1_g[0][None, :]
    (rec_c,) = _inproj(ctx2d, mod, g1n, w_in16, lb, mod_row=1, full=False, tm=t_ctx["ctx_tm"])
    s_zero = jnp.zeros((2, N_HEADS, HEAD_DIM, HEAD_DIM), F32)
    _, _, s_ctx = _scan(rec_c, s_zero, t_ctx["scan_chunks"])
    rec, rest = _inproj(x2d, mod, g1n, w_in16, lb, mod_row=0, full=True, tm=t_lat["inproj_tm"])
    o_f, o_b, _ = _scan(rec, s_ctx, t_lat["scan_chunks"])
    x1, h2 = _mixout(o_f, o_b, rest, x2d, mod, norm2_g[0][None, :], a_norm_g[0][None, :], sconv_w[0],
                     w_pa[0].astype(BF16), w_pb[0].astype(BF16), w_o[0].astype(BF16), tm=t_lat["mix_tm"])
    out = _ffn(h2, x1, mod, w_up[0].astype(BF16), w_down[0].astype(BF16), ffn_dw[0].reshape(9, D_FF),
               ffn_db[0][None, :], final_g[None, :], tm=t_lat["ffn_tm"], tf=t_lat["ffn_tf"], sub=t_lat["ffn_sub"])
    return out[None]
```

```python
import functools
import math

import jax
import jax.numpy as jnp
from jax import lax
from jax.experimental import pallas as pl
from jax.experimental.pallas import tpu as pltpu

D_MODEL = 2048
N_HEADS = 8
HEAD_DIM = 128
A_WIDTH = N_HEADS * HEAD_DIM
B_WIDTH = 1024
D_FF = 5632
CHUNK = 64
GRID_W = 64
EPS = 1e-6
SEG = 1024
N_SEG_REC = 4
SEG_H = 7
REST_GATE_A, REST_GATE_B, REST_G, REST_B, REST_U, N_REST = 0, 2, 4, 5, 6, 7
V7X_VMEM_BYTES = 64 * 1024 * 1024
VMEM_LIMIT = V7X_VMEM_BYTES - 8 * 1024 * 1024
BF16_ROWS = 16
LOG2E = math.log2(math.e)
BF16 = jnp.bfloat16
F32 = jnp.float32


def _tiles(L):
    return dict(inproj_tm=min(L, 1024), mix_tm=min(L, 256), ctx_tm=min(L, 256), ffn_tm=min(L, 512), ffn_tf=512, ffn_sub=256,
                scan_chunks=min(L // CHUNK, 8), mod_tn=512)


def _sigmoid(z):
    return 0.5 * jnp.tanh(0.5 * z) + 0.5


def _silu(z):
    return z * _sigmoid(z)


def _const_spec(shape, index=None):
    idx = index if index is not None else (0,) * len(shape)
    return pl.BlockSpec(shape, lambda *_: idx, pipeline_mode=pl.Buffered(1))


def _params(*semantics):
    return pltpu.CompilerParams(dimension_semantics=semantics, vmem_limit_bytes=VMEM_LIMIT)


def _mod_kernel(cc_ref, w_ref, b_ref, o_ref):
    s = _silu(cc_ref[...]).astype(BF16)
    o_ref[...] = jnp.dot(s, w_ref[...].astype(BF16), preferred_element_type=F32) + b_ref[...]


def _modulation(cc, w_mod, b_mod, tn):
    rows, d = cc.shape
    n = w_mod.shape[1]
    return pl.pallas_call(
        _mod_kernel,
        grid=(n // tn,),
        in_specs=[pl.BlockSpec((rows, d), lambda j: (0, 0)),
                  pl.BlockSpec((d, tn), lambda j: (0, j)),
                  pl.BlockSpec((1, tn), lambda j: (0, j))],
        out_specs=pl.BlockSpec((rows, tn), lambda j: (0, j)),
        out_shape=jax.ShapeDtypeStruct((rows, n), F32),
        compiler_params=_params("arbitrary"),
        name="modulation",
    )(cc, w_mod, b_mod)


def _lb_kernel(raw_ref, o_ref):
    raw = raw_ref[...]
    e = jnp.exp(raw - jnp.max(raw, axis=0, keepdims=True))
    o_ref[...] = e[0:1, :] / jnp.sum(e, axis=0, keepdims=True)


def _lower_bounds(lb_raw2d):
    return pl.pallas_call(
        _lb_kernel,
        out_shape=jax.ShapeDtypeStruct((1, lb_raw2d.shape[1]), F32),
        name="lower_bounds",
    )(lb_raw2d)


def _inproj_kernel(*refs, mod_row, full):
    if full:
        x_ref, sh_ref, sc_ref, g_ref, w_ref, wh_ref, lb_ref, rec_ref, rest_ref, hx_ref = refs
    else:
        x_ref, sh_ref, sc_ref, g_ref, w_ref, lb_ref, rec_ref, hx_ref = refs
    j = pl.program_id(1)

    @pl.when(j == 0)
    def _():
        x = x_ref[...]
        y = x * lax.rsqrt(jnp.mean(x * x, axis=-1, keepdims=True) + EPS) * g_ref[...]
        sh = sh_ref[mod_row:mod_row + 1, :]
        sc = sc_ref[mod_row:mod_row + 1, :]
        hx_ref[...] = (y * (1.0 + sc) + sh).astype(BF16)

    tm = hx_ref.shape[0]
    halves = [slice(r * (tm // 2), (r + 1) * (tm // 2)) for r in range(2)]

    def z(rows, wr=w_ref):
        return jnp.dot(hx_ref[rows, :], wr[...], preferred_element_type=F32)

    @pl.when(j == 0)
    def _():
        for rows in halves:
            rec_ref[rows, :] = _silu(z(rows))

    @pl.when((j == 1) | (j == 2))
    def _():
        lb = jnp.where(j == 1, lb_ref[:, 0:SEG], lb_ref[:, SEG:2 * SEG])
        for rows in halves:
            rec_ref[rows, :] = lb + (1.0 - lb) * _sigmoid(z(rows))

    @pl.when(j == 3)
    def _():
        for rows in halves:
            rec_ref[rows, :] = z(rows)

    if full:
        @pl.when(j == 4)
        def _():
            for rows in halves:
                rest_ref[rows, :] = _silu(z(rows)).astype(BF16)

        @pl.when(j == 5)
        def _():
            for rows in halves:
                rest_ref[rows, :] = z(rows).astype(BF16)

        @pl.when(j == 6)
        def _():
            for rows in halves:
                rest_ref[rows, :] = (z(rows) * z(rows, wh_ref)).astype(BF16)

        @pl.when(j >= 7)
        def _():
            for rows in halves:
                rest_ref[rows, :] = _sigmoid(z(rows)).astype(BF16)


def _inproj(x2d, mod, g, w, lb, *, mod_row, full, tm):
    L, d = x2d.shape
    n_steps = 11 if full else N_SEG_REC
    w_seg = lambda i, j: (0, jnp.where(j >= SEG_H, j + 1, j))
    rest_blk = lambda i, j: (i, jnp.where(j < REST_G, REST_G, jnp.where(j <= REST_U, j, j - 7)))
    in_specs = [pl.BlockSpec((tm, d), lambda i, j: (i, 0)),
                pl.BlockSpec((8, d), lambda i, j: (0, 0)),
                pl.BlockSpec((8, d), lambda i, j: (0, 1)),
                pl.BlockSpec((1, d), lambda i, j: (0, 0)),
                pl.BlockSpec((d, SEG), w_seg)]
    args = [x2d, mod, mod, g, w]
    out_specs = [pl.BlockSpec((tm, SEG), lambda i, j: (i, jnp.minimum(j, N_SEG_REC - 1)))]
    out_shape = [jax.ShapeDtypeStruct((L, N_SEG_REC * SEG), F32)]
    if full:
        in_specs.append(_const_spec((d, SEG), (0, SEG_H)))
        args.append(w)
        out_specs.append(pl.BlockSpec((tm, SEG), rest_blk))
        out_shape.append(jax.ShapeDtypeStruct((L, N_REST * SEG), BF16))
    in_specs.append(pl.BlockSpec((1, 2 * SEG), lambda i, j: (0, 0)))
    args.append(lb)
    return pl.pallas_call(
        functools.partial(_inproj_kernel, mod_row=mod_row, full=full),
        grid=(L // tm, n_steps),
        in_specs=in_specs,
        out_specs=out_specs,
        out_shape=out_shape,
        scratch_shapes=[pltpu.VMEM((tm, d), BF16)],
        compiler_params=_params("arbitrary", "arbitrary"),
        name="inproj_full" if full else "inproj_rec",
    )(*args)


LEVEL_HALVES = (32, 16, 8, 4, 2, 1)


def _scan_consts():
    t = jnp.arange(CHUNK)
    tril = (t[:, None] >= t[None, :])
    cums = jnp.stack([tril, tril.T]).astype(BF16)
    masks = []
    for half in LEVEL_HALVES:
        blk = 2 * half
        same = (t[:, None] // blk) == (t[None, :] // blk)
        fwd = same & ((t[:, None] % blk) >= half) & ((t[None, :] % blk) < half)
        masks.append(jnp.stack([fwd, fwd.T]))
    masks = jnp.stack(masks, axis=1).astype(F32)
    return cums, masks


def _block_reference_rows(b, half, ridx):
    blk = 2 * half
    n = CHUNK // blk
    width = b.shape[1]
    if blk >= 8:
        parts = [jnp.broadcast_to(b[i * blk + ridx:i * blk + ridx + 1, :], (blk, width)) for i in range(n)]
        return parts[0] if n == 1 else jnp.concatenate(parts, axis=0)
    pos = lax.broadcasted_iota(jnp.int32, b.shape, 0) & (blk - 1)
    r = b
    for p in range(blk):
        delta = ridx - p
        if delta != 0:
            r = jnp.where(pos == p, pltpu.roll(b, (-delta) % CHUNK, axis=0), r)
    return r


def _neg_abs(x):
    bits = lax.bitcast_convert_type(x, jnp.uint32) | jnp.uint32(0x80000000)
    return lax.bitcast_convert_type(bits, F32)


def _query_key_rows(rev, half, q, k):
    blk = 2 * half
    if half >= 8:
        parts = []
        for i in range(CHUNK // blk):
            lo, mid, hi = i * blk, i * blk + half, (i + 1) * blk
            parts += [q[lo:mid], k[mid:hi]] if rev else [k[lo:mid], q[mid:hi]]
        return jnp.concatenate(parts, axis=0)
    pos = lax.broadcasted_iota(jnp.int32, q.shape, 0) & (blk - 1)
    return jnp.where((pos < half) if rev else (pos >= half), q, k)


def _scan_cumulative(d, g_ref, rows, cum, b_ref):
    g = jnp.log2(g_ref[rows, :])
    g_hi = g.astype(BF16)
    g_lo = (g - g_hi.astype(F32)).astype(BF16)
    b_ref[d] = jnp.dot(cum, g_hi, preferred_element_type=F32) + jnp.dot(cum, g_lo, preferred_element_type=F32)


def _scan_scores(d, h, q_ref, g_ref, v_ref, rows, masks_ref, b_ref, pos):
    rev = d == 1
    sl = slice(h * HEAD_DIM, (h + 1) * HEAD_DIM)
    end = 0 if rev else CHUNK - 1
    q = q_ref[rows, sl]
    v = v_ref[rows, sl]
    f = g_ref[rows, sl]
    k = 1.0 - f
    qf = q * f
    b = b_ref[d, :, sl]
    b_end = b_ref[d, end:end + 1, sl]
    scores = None
    for lvl, half in enumerate(LEVEL_HALVES):
        if half > 2:
            r = _block_reference_rows(b, half, half if rev else half - 1)
            xl = _query_key_rows(rev, half, q, k) * jnp.exp2(_neg_abs(b - r))
        elif half == 2:
            f_prev = pltpu.roll(f, 1, axis=0)
            f_next = pltpu.roll(f, CHUNK - 1, axis=0)
            if rev:
                xl = jnp.where(pos["q4"], qf * jnp.where(pos["first4"], f_next, 1.0), k * jnp.where(pos["last4"], f_prev, 1.0))
            else:
                xl = jnp.where(pos["q4"], qf * jnp.where(pos["last4"], f_prev, 1.0), k * jnp.where(pos["first4"], f_next, 1.0))
        else:
            xl = jnp.where(pos["q2"], qf, k)
        xl = xl.astype(BF16)
        a = lax.dot_general(xl, xl, (((1,), (1,)), ((), ())), preferred_element_type=F32) * masks_ref[d, lvl]
        scores = a if scores is None else scores + a
    qb = (q * jnp.exp2(b)).astype(BF16)
    kdec = (k * jnp.exp2(b_end - b)).astype(BF16)
    own = jnp.sum(q * k, axis=-1, keepdims=True) * v
    return scores.astype(BF16), qb, kdec, v.astype(BF16), own, jnp.exp2(b_end)


def _scan_state(d, h, staged, rows, s_ref, o_ref):
    p16, qb, kdec, v16, own, decay = staged
    sl = slice(h * HEAD_DIM, (h + 1) * HEAD_DIM)
    st = s_ref[d, h]
    o_inter = lax.dot_general(qb, st.astype(BF16), (((1,), (1,)), ((), ())), preferred_element_type=F32)
    o_intra = jnp.dot(p16, v16, preferred_element_type=F32)
    o_ref[rows, sl] = (o_inter + o_intra + own).astype(o_ref.dtype)
    upd = lax.dot_general(v16, kdec, (((0,), (0,)), ((), ())), preferred_element_type=F32)
    s_ref[d, h] = st * decay + upd


SCAN_LAG = 3


def _scan_kernel(qf_ref, gf_ref, vf_ref, qb_ref, gb_ref, vb_ref, s0_ref, cum_ref, masks_ref,
                 of_ref, ob_ref, sfin_ref, s_ref, b_ref, *, n_sub):
    c = pl.program_id(0)

    @pl.when(c == 0)
    def _():
        s_ref[...] = s0_ref[...]

    def body(cc, carry):
        rows = (pl.ds(pl.multiple_of(cc * CHUNK, CHUNK), CHUNK),
                pl.ds(pl.multiple_of((n_sub - 1 - cc) * CHUNK, CHUNK), CHUNK))
        ins = ((qf_ref, gf_ref, vf_ref), (qb_ref, gb_ref, vb_ref))
        outs = (of_ref, ob_ref)
        for d in range(2):
            _scan_cumulative(d, ins[d][1], rows[d], cum_ref[d], b_ref)
        r4 = lax.broadcasted_iota(jnp.int32, (CHUNK, HEAD_DIM), 0) & 3
        pos = [dict(q4=(r4 < 2) if d else (r4 >= 2), first4=r4 == 0, last4=r4 == 3,
                    q2=((r4 & 1) == 0) if d else ((r4 & 1) == 1)) for d in range(2)]
        chains = [(d, h) for h in range(N_HEADS) for d in range(2)]
        staged = {}
        for n in range(len(chains) + SCAN_LAG):
            if n < len(chains):
                d, h = chains[n]
                staged[n] = _scan_scores(d, h, *ins[d], rows[d], masks_ref, b_ref, pos[d])
            if n >= SCAN_LAG:
                d, h = chains[n - SCAN_LAG]
                _scan_state(d, h, staged.pop(n - SCAN_LAG), rows[d], s_ref, outs[d])
        return carry

    lax.fori_loop(0, n_sub, body, 0)

    @pl.when(c == pl.num_programs(0) - 1)
    def _():
        sfin_ref[...] = s_ref[...]


def _scan(rec, s0, n_sub):
    L = rec.shape[0]
    ns = L // (CHUNK * n_sub)
    cums, masks = _scan_consts()
    blk = (CHUNK * n_sub, SEG)
    state_spec = pl.BlockSpec(s0.shape, lambda c: (0, 0, 0, 0))
    return pl.pallas_call(
        functools.partial(_scan_kernel, n_sub=n_sub),
        grid=(ns,),
        in_specs=[pl.BlockSpec(blk, lambda c: (c, 0)),
                  pl.BlockSpec(blk, lambda c: (c, 1)),
                  pl.BlockSpec(blk, lambda c: (c, 3)),
                  pl.BlockSpec(blk, lambda c: (ns - 1 - c, 0)),
                  pl.BlockSpec(blk, lambda c: (ns - 1 - c, 2)),
                  pl.BlockSpec(blk, lambda c: (ns - 1 - c, 3)),
                  state_spec,
                  pl.BlockSpec(cums.shape, lambda c: (0, 0, 0)),
                  pl.BlockSpec(masks.shape, lambda c: (0, 0, 0, 0))],
        out_specs=[pl.BlockSpec(blk, lambda c: (c, 0)),
                   pl.BlockSpec(blk, lambda c: (ns - 1 - c, 0)),
                   state_spec],
        out_shape=[jax.ShapeDtypeStruct((L, A_WIDTH), BF16),
                   jax.ShapeDtypeStruct((L, A_WIDTH), BF16),
                   jax.ShapeDtypeStruct(s0.shape, F32)],
        scratch_shapes=[pltpu.VMEM(s0.shape, F32), pltpu.VMEM((2, CHUNK, A_WIDTH), F32)],
        compiler_params=_params("arbitrary"),
        name=f"scan_{ns}",
    )(rec, rec, rec, rec, rec, rec, s0, cums, masks)


def _mixout_kernel(of_ref, ob_ref, sg_ref, sb_ref, u_ref, up_ref, un_ref, ga_ref, gb_ref, x_ref,
                   g1_ref, sh2_ref, sc2_ref, n2_ref, ag_ref, cw_ref, wpa_ref, wpb_ref, wo_ref,
                   x1_ref, h2_ref, ya_ref):
    i = pl.program_id(0)
    tm = x_ref.shape[0]
    for h in range(N_HEADS):
        sl = slice(h * HEAD_DIM, (h + 1) * HEAD_DIM)
        o = of_ref[:, sl].astype(F32) + ob_ref[:, sl].astype(F32)
        o = o * lax.rsqrt(jnp.mean(o * o, axis=-1, keepdims=True) + EPS) * ag_ref[...]
        ya_ref[:, sl] = (o * sg_ref[:, sl].astype(F32)).astype(BF16)
    u = u_ref[...].astype(F32)
    rows = lax.broadcasted_iota(jnp.int32, u.shape, 0)
    prev_row = jnp.where(i > 0, up_ref[...].astype(F32)[BF16_ROWS - 1:BF16_ROWS, :], 0.0)
    next_row = jnp.where(i < pl.num_programs(0) - 1, un_ref[...].astype(F32)[0:1, :], 0.0)
    u_prev = jnp.where(rows == 0, prev_row, pltpu.roll(u, 1, axis=0))
    u_next = jnp.where(rows == tm - 1, next_row, pltpu.roll(u, tm - 1, axis=0))
    yb = sb_ref[...].astype(F32) * (cw_ref[0:1, :] * u_prev + cw_ref[1:2, :] * u + cw_ref[2:3, :] * u_next)
    pa = jnp.dot(ya_ref[...], wpa_ref[...], preferred_element_type=F32)
    pb = jnp.dot(yb.astype(BF16), wpb_ref[...], preferred_element_type=F32)
    merged = (ga_ref[...].astype(F32) * pa + gb_ref[...].astype(F32) * pb).astype(BF16)
    out = jnp.dot(merged, wo_ref[...], preferred_element_type=F32)
    x1 = x_ref[...] + g1_ref[0:1, :] * out
    x1_ref[...] = x1
    y = x1 * lax.rsqrt(jnp.mean(x1 * x1, axis=-1, keepdims=True) + EPS) * n2_ref[...]
    h2_ref[...] = (y * (1.0 + sc2_ref[0:1, :]) + sh2_ref[0:1, :]).astype(BF16)


def _mixout(o_f, o_b, rest, x2d, mod, n2g, ag, cw, wpa, wpb, wo, *, tm):
    L, d = x2d.shape
    nt = L // tm
    rt = tm // BF16_ROWS
    last_t = L // BF16_ROWS - 1
    row = lambda i: (i, 0)
    return pl.pallas_call(
        _mixout_kernel,
        grid=(nt,),
        in_specs=[pl.BlockSpec((tm, A_WIDTH), row),
                  pl.BlockSpec((tm, A_WIDTH), row),
                  pl.BlockSpec((tm, SEG), lambda i: (i, REST_G)),
                  pl.BlockSpec((tm, SEG), lambda i: (i, REST_B)),
                  pl.BlockSpec((tm, SEG), lambda i: (i, REST_U)),
                  pl.BlockSpec((BF16_ROWS, SEG), lambda i: (jnp.maximum(i * rt - 1, 0), REST_U)),
                  pl.BlockSpec((BF16_ROWS, SEG), lambda i: (jnp.minimum((i + 1) * rt, last_t), REST_U)),
                  pl.BlockSpec((tm, d), lambda i: (i, REST_GATE_A // 2)),
                  pl.BlockSpec((tm, d), lambda i: (i, REST_GATE_B // 2)),
                  pl.BlockSpec((tm, d), row),
                  pl.BlockSpec((8, d), lambda i: (0, 2)),
                  pl.BlockSpec((8, d), lambda i: (0, 3)),
                  pl.BlockSpec((8, d), lambda i: (0, 4)),
                  pl.BlockSpec((1, d), lambda i: (0, 0)),
                  pl.BlockSpec((1, HEAD_DIM), lambda i: (0, 0)),
                  pl.BlockSpec(cw.shape, lambda i: (0, 0)),
                  _const_spec(wpa.shape), _const_spec(wpb.shape), _const_spec(wo.shape)],
        out_specs=[pl.BlockSpec((tm, d), row), pl.BlockSpec((tm, d), row)],
        out_shape=[jax.ShapeDtypeStruct((L, d), F32), jax.ShapeDtypeStruct((L, d), BF16)],
        scratch_shapes=[pltpu.VMEM((tm, A_WIDTH), BF16)],
        compiler_params=_params("arbitrary"),
        name="mixer_out",
    )(o_f, o_b, rest, rest, rest, rest, rest, rest, rest, x2d, mod, mod, mod, n2g, ag, cw, wpa, wpb, wo)


def _ffn_kernel(h_ref, hb_ref, x1_ref, wa_ref, wb_ref, wd_ref, dw_ref, db_ref, g2_ref, fg_ref,
                o_ref, hx_ref, edge_ref, *, sub):
    i = pl.program_id(0)
    f = pl.program_id(1)
    tm = h_ref.shape[0]
    tf = wa_ref.shape[1]
    ext = tm + 2 * GRID_W

    @pl.when(f == 0)
    def _():
        hx_ref[0:tm, :] = h_ref[...]
        hx_ref[tm:tm + GRID_W, :] = jnp.where(i < pl.num_programs(0) - 1, hb_ref[...], jnp.zeros(hb_ref.shape, BF16))
        o_ref[...] = jnp.zeros(o_ref.shape, F32)

    @pl.when((f == 0) & (i == 0))
    def _():
        edge_ref[...] = jnp.zeros(edge_ref.shape, F32)

    ng = tm // GRID_W
    n_sub = tf // sub
    a_all = [jnp.dot(hx_ref[...], wa_ref[:, s * sub:(s + 1) * sub], preferred_element_type=F32) for s in range(n_sub)]
    b_all = [jnp.dot(hx_ref[0:tm, :], wb_ref[:, s * sub:(s + 1) * sub], preferred_element_type=F32)
             for s in range(n_sub)]
    colg = lax.broadcasted_iota(jnp.int32, (GRID_W, sub), 0)
    acc = None
    for s in range(n_sub):
        cs = slice(s * sub, (s + 1) * sub)
        a = jnp.concatenate([edge_ref[f, :, cs], a_all[s]], axis=0)
        edge_ref[f, :, cs] = a_all[s][tm - GRID_W:tm]
        a3 = a.reshape(ng + 2, GRID_W, sub)
        l3 = pltpu.roll(a, 1, axis=0).reshape(ng + 2, GRID_W, sub)
        r3 = pltpu.roll(a, ext - 1, axis=0).reshape(ng + 2, GRID_W, sub)
        conv = None
        for dr in range(3):
            w_l = jnp.where(colg == 0, 0.0, dw_ref[3 * dr:3 * dr + 1, cs])
            w_c = dw_ref[3 * dr + 1:3 * dr + 2, cs]
            w_r = jnp.where(colg == GRID_W - 1, 0.0, dw_ref[3 * dr + 2:3 * dr + 3, cs])
            term = w_l * l3[dr:dr + ng] + w_c * a3[dr:dr + ng] + w_r * r3[dr:dr + ng]
            conv = term if conv is None else conv + term
        hid = (_silu(conv + db_ref[:, cs]) * b_all[s].reshape(ng, GRID_W, sub)).astype(BF16).reshape(tm, sub)
        part = jnp.dot(hid, wd_ref[cs, :], preferred_element_type=F32)
        acc = part if acc is None else acc + part
    o_ref[...] += acc

    @pl.when(f == pl.num_programs(1) - 1)
    def _():
        x2 = x1_ref[...] + g2_ref[0:1, :] * o_ref[...]
        o_ref[...] = x2 * lax.rsqrt(jnp.mean(x2 * x2, axis=-1, keepdims=True) + EPS) * fg_ref[...]


def _ffn(h2, x1, mod, w_up, w_down, dw9, db, fg, *, tm, tf, sub):
    L, d = x1.shape
    nt = L // tm
    nf = D_FF // tf
    rg = tm // GRID_W
    lastg = L // GRID_W - 1
    return pl.pallas_call(
        functools.partial(_ffn_kernel, sub=sub),
        grid=(nt, nf),
        in_specs=[pl.BlockSpec((tm, d), lambda i, f: (i, 0)),
                  pl.BlockSpec((GRID_W, d), lambda i, f: (jnp.minimum((i + 1) * rg, lastg), 0)),
                  pl.BlockSpec((tm, d), lambda i, f: (i, 0)),
                  pl.BlockSpec((d, tf), lambda i, f: (0, f)),
                  pl.BlockSpec((d, tf), lambda i, f: (0, nf + f)),
                  pl.BlockSpec((tf, d), lambda i, f: (f, 0)),
                  pl.BlockSpec((9, tf), lambda i, f: (0, f)),
                  pl.BlockSpec((1, tf), lambda i, f: (0, f)),
                  pl.BlockSpec((8, d), lambda i, f: (0, 5)),
                  pl.BlockSpec((1, d), lambda i, f: (0, 0))],
        out_specs=pl.BlockSpec((tm, d), lambda i, f: (i, 0)),
        out_shape=jax.ShapeDtypeStruct((L, d), F32),
        scratch_shapes=[pltpu.VMEM((tm + GRID_W, d), BF16),
                        pltpu.VMEM((nf, GRID_W, tf), F32)],
        compiler_params=_params("arbitrary", "arbitrary"),
        name="conv_ffn",
    )(h2, h2, x1, w_up, w_up, w_down, dw9, db, mod, fg)


def kernel(x, c, ctx, c_ctx, w_mod, b_mod, norm1_g, w_in, lb_raw, a_norm_g, sconv_w, w_pa, w_pb, w_o,
           norm2_g, w_up, ffn_dw, ffn_db, w_down, final_g):
    assert x.shape[0] == 1 and w_in.shape[0] == 1, "single-sequence, depth-1 layer"
    d = D_MODEL
    x2d = x[0]
    ctx2d = ctx[0]
    t_lat = _tiles(x2d.shape[0])
    t_ctx = _tiles(ctx2d.shape[0])

    cc = jnp.zeros((16, d), F32).at[0].set(c[0]).at[1].set(c_ctx)
    mod = _modulation(cc, w_mod[0], b_mod[0][None, :], t_lat["mod_tn"])
    lb = _lower_bounds(lb_raw.reshape(lb_raw.shape[0], 2 * A_WIDTH))

    w_in16 = w_in[0].astype(BF16)
    g1n = norm1_g[0][None, :]
    (rec_c,) = _inproj(ctx2d, mod, g1n, w_in16, lb, mod_row=1, full=False, tm=t_ctx["ctx_tm"])
    s_zero = jnp.zeros((2, N_HEADS, HEAD_DIM, HEAD_DIM), F32)
    _, _, s_ctx = _scan(rec_c, s_zero, t_ctx["scan_chunks"])
    rec, rest = _inproj(x2d, mod, g1n, w_in16, lb, mod_row=0, full=True, tm=t_lat["inproj_tm"])
    o_f, o_b, _ = _scan(rec, s_ctx, t_lat["scan_chunks"])
    x1, h2 = _mixout(o_f, o_b, rest, x2d, mod, norm2_g[0][None, :], a_norm_g[0][None, :], sconv_w[0],
                     w_pa[0].astype(BF16), w_pb[0].astype(BF16), w_o[0].astype(BF16), tm=t_lat["mix_tm"])
    out = _ffn(h2, x1, mod, w_up[0].astype(BF16), w_down[0].astype(BF16), ffn_dw[0].reshape(9, D_FF),
               ffn_db[0][None, :], final_g[None, :], tm=t_lat["ffn_tm"], tf=t_lat["ffn_tf"], sub=t_lat["ffn_sub"])
    return out[None]
```
